```python
import math
import jax, jax.numpy as jnp
from jax import lax
import numpy as np

D_MODEL = 1024
BATCH = 4
SEQ = 4096
DEPTH = 2

N_A = DEPTH // 2
N_B = DEPTH - N_A
CONV_W = 3
N_HEADS = 8
HEAD_DIM = D_MODEL // N_HEADS
BLOCK = 256
TOPK = 3
Q_CHUNK = 32
ROT_DIM = HEAD_DIM // 4
ROPE_THETA = 500000.0
MEM_LEN = 256
MEM_HEADS = 4
MEM_HEAD_DIM = D_MODEL // MEM_HEADS
D_FF = ((8 * D_MODEL + 3 * 256 - 1) // (3 * 256)) * 256
EPS = 1e-6

kernel_name = "yoco_shortconv_moba_hybrid"


def rms_norm(x, g):
    x32 = x.astype(jnp.float32)
    y = x32 * lax.rsqrt(jnp.mean(x32 * x32, axis=-1, keepdims=True) + EPS)
    return (y * g.astype(jnp.float32)).astype(x.dtype)


def rope_cos_sin(positions):
    inv_freq = ROPE_THETA ** (-jnp.arange(0, ROT_DIM, 2, dtype=jnp.float32) / ROT_DIM)
    ang = positions.astype(jnp.float32)[..., None] * inv_freq
    return jnp.cos(ang)[:, :, None, :], jnp.sin(ang)[:, :, None, :]


def apply_partial_rope(x, cos, sin):
    xr = x[..., :ROT_DIM].astype(jnp.float32)
    x1, x2 = xr[..., : ROT_DIM // 2], xr[..., ROT_DIM // 2:]
    rot = jnp.concatenate([x1 * cos - x2 * sin, x2 * cos + x1 * sin], axis=-1)
    return jnp.concatenate([rot.astype(x.dtype), x[..., ROT_DIM:]], axis=-1)


def short_conv_mixer(h, w_in, w_conv, w_out):
    b_gate, c_gate, u = jnp.split(h @ w_in, 3, axis=-1)
    z = c_gate * u
    conv = lax.conv_general_dilated(
        z, w_conv[:, None, :].astype(z.dtype), window_strides=(1,),
        padding=[(CONV_W - 1, 0)], dimension_numbers=("NWC", "WIO", "NWC"),
        feature_group_count=D_MODEL)
    return (b_gate * conv) @ w_out


def mem_cross_attention(h, mem_n, w_q, w_kv, w_o):
    bsz, seq, _ = h.shape
    q = (h @ w_q).reshape(bsz, seq, MEM_HEADS, MEM_HEAD_DIM)
    k, v = jnp.split(mem_n @ w_kv, 2, axis=-1)
    k = k.reshape(bsz, -1, MEM_HEADS, MEM_HEAD_DIM)
    v = v.reshape(bsz, -1, MEM_HEADS, MEM_HEAD_DIM)
    s = jnp.einsum("bshd,bmhd->bhsm", q, k).astype(jnp.float32) * (MEM_HEAD_DIM ** -0.5)
    p = jax.nn.softmax(s, axis=-1).astype(v.dtype)
    o = jnp.einsum("bhsm,bmhd->bshd", p, v).reshape(bsz, seq, D_MODEL)
    return o @ w_o


def swiglu(h, w_gu, w_down):
    g, u = jnp.split(h @ w_gu, 2, axis=-1)
    return (jax.nn.silu(g) * u) @ w_down


def shared_kv(x, kv_norm, w_kv, cos, sin):
    bsz, seq, _ = x.shape
    n_blocks = -(-seq // BLOCK)
    pad = n_blocks * BLOCK - seq
    k, v = jnp.split(rms_norm(x, kv_norm) @ w_kv, 2, axis=-1)
    k = apply_partial_rope(k.reshape(bsz, seq, N_HEADS, HEAD_DIM), cos, sin)
    v = v.reshape(bsz, seq, N_HEADS, HEAD_DIM)
    k = jnp.pad(k, ((0, 0), (0, pad), (0, 0), (0, 0)))
    v = jnp.pad(v, ((0, 0), (0, pad), (0, 0), (0, 0)))
    k_blk = k.transpose(0, 2, 1, 3).reshape(bsz, N_HEADS, n_blocks, BLOCK, HEAD_DIM)
    v_blk = v.transpose(0, 2, 1, 3).reshape(bsz, N_HEADS, n_blocks, BLOCK, HEAD_DIM)
    k_mean = jnp.mean(k_blk.astype(jnp.float32), axis=3)
    return k_blk, v_blk, k_mean


def moba_attention(h, w_q, w_o, cos, sin, k_blk, v_blk, k_mean):
    bsz, seq, _ = h.shape
    n_blocks = k_blk.shape[2]
    k_sel = min(TOPK, n_blocks)
    n_chunks = seq // Q_CHUNK
    scale = HEAD_DIM ** -0.5
    q = apply_partial_rope((h @ w_q).reshape(bsz, seq, N_HEADS, HEAD_DIM), cos, sin)
    q_chunks = q.transpose(0, 2, 1, 3).reshape(bsz, N_HEADS, n_chunks, Q_CHUNK, HEAD_DIM)
    q_chunks = q_chunks.transpose(2, 0, 1, 3, 4)
    bi = jnp.arange(bsz)[:, None, None, None]
    hi = jnp.arange(N_HEADS)[None, :, None, None]

    def chunk_fn(args):
        c, q_c = args
        qpos = c * Q_CHUNK + jnp.arange(Q_CHUNK)
        qb = (c * Q_CHUNK) // BLOCK
        gate = jnp.einsum("bhqd,bhnd->bhqn", q_c.astype(jnp.float32), k_mean)
        past = jnp.arange(n_blocks) < qb
        gate = jnp.where(past[None, None, None, :], gate, -jnp.inf)
        _, idx = lax.top_k(gate, k_sel)
        valid = idx < qb
        kg = k_blk[bi, hi, idx]
        vg = v_blk[bi, hi, idx]
        s_sel = jnp.einsum("bhqd,bhqrkd->bhqrk", q_c, kg).astype(jnp.float32) * scale
        s_sel = jnp.where(valid[..., None], s_sel, -jnp.inf)
        s_sel = s_sel.reshape(bsz, N_HEADS, Q_CHUNK, k_sel * BLOCK)
        k_own = lax.dynamic_index_in_dim(k_blk, qb, axis=2, keepdims=False)
        v_own = lax.dynamic_index_in_dim(v_blk, qb, axis=2, keepdims=False)
        s_own = jnp.einsum("bhqd,bhkd->bhqk", q_c, k_own).astype(jnp.float32) * scale
        kpos = qb * BLOCK + jnp.arange(BLOCK)
        s_own = jnp.where((kpos[None, :] <= qpos[:, None])[None, None], s_own, -jnp.inf)
        p = jax.nn.softmax(jnp.concatenate([s_sel, s_own], axis=-1), axis=-1)
        p_sel = p[..., : k_sel * BLOCK].reshape(bsz, N_HEADS, Q_CHUNK, k_sel, BLOCK)
        p_own = p[..., k_sel * BLOCK:]
        o = jnp.einsum("bhqrk,bhqrkd->bhqd", p_sel.astype(vg.dtype), vg)
        o = o + jnp.einsum("bhqk,bhkd->bhqd", p_own.astype(v_own.dtype), v_own)
        return o

    out = lax.map(chunk_fn, (jnp.arange(n_chunks, dtype=jnp.int32), q_chunks))
    out = out.transpose(1, 0, 3, 2, 4).reshape(bsz, seq, D_MODEL)
    return out @ w_o


def setup_inputs(seed: int = 0) -> dict:
    key = jax.random.key(seed)
    ks = jax.random.split(key, 24)
    f32 = jnp.float32

    def w(k, shape, fan_in):
        return jax.random.normal(k, shape, f32) * (fan_in ** -0.5)

    def gain(k, shape):
        return 1.0 + 0.02 * jax.random.normal(k, shape, f32)

    x = jax.random.normal(ks[0], (BATCH, SEQ, D_MODEL), f32)
    mem = jax.random.normal(ks[1], (BATCH, MEM_LEN, D_MODEL), f32)
    offset = jax.random.randint(ks[2], (BATCH, 1), 0, 8192, dtype=jnp.int32)
    positions = offset + jnp.arange(SEQ, dtype=jnp.int32)[None, :]
    return {
        "x": x,
        "mem": mem,
        "positions": positions,
        "norm_mix": gain(ks[3], (DEPTH, D_MODEL)),
        "norm_mem": gain(ks[4], (DEPTH, D_MODEL)),
        "norm_memkv": gain(ks[5], (DEPTH, D_MODEL)),
        "norm_ffn": gain(ks[6], (DEPTH, D_MODEL)),
        "norm_final": gain(ks[7], (D_MODEL,)),
        "conv_w_in": w(ks[8], (N_A, D_MODEL, 3 * D_MODEL), D_MODEL),
        "conv_w": w(ks[9], (N_A, CONV_W, D_MODEL), CONV_W),
        "conv_w_out": w(ks[10], (N_A, D_MODEL, D_MODEL), D_MODEL),
        "kv_norm": gain(ks[11], (D_MODEL,)),
        "w_kv": w(ks[12], (D_MODEL, 2 * D_MODEL), D_MODEL),
        "moba_w_q": w(ks[13], (N_B, D_MODEL, D_MODEL), D_MODEL),
        "moba_w_o": w(ks[14], (N_B, D_MODEL, D_MODEL), D_MODEL),
        "mem_w_q": w(ks[15], (DEPTH, D_MODEL, D_MODEL), D_MODEL),
        "mem_w_kv": w(ks[16], (DEPTH, D_MODEL, 2 * D_MODEL), D_MODEL),
        "mem_w_o": w(ks[17], (DEPTH, D_MODEL, D_MODEL), D_MODEL),
        "ffn_w_gu": w(ks[18], (DEPTH, D_MODEL, 2 * D_FF), D_MODEL),
        "ffn_w_down": w(ks[19], (DEPTH, D_FF, D_MODEL), D_FF),
    }


def reference(x, mem, positions, norm_mix, norm_mem, norm_memkv, norm_ffn, norm_final,
              conv_w_in, conv_w, conv_w_out, kv_norm, w_kv, moba_w_q, moba_w_o,
              mem_w_q, mem_w_kv, mem_w_o, ffn_w_gu, ffn_w_down):
    cos, sin = rope_cos_sin(positions)
    k_blk = v_blk = k_mean = None
    for l in range(DEPTH):
        h = rms_norm(x, norm_mix[l])
        if l < N_A:
            x = x + short_conv_mixer(h, conv_w_in[l], conv_w[l], conv_w_out[l])
        else:
            j = l - N_A
            x = x + moba_attention(h, moba_w_q[j], moba_w_o[j], cos, sin, k_blk, v_blk, k_mean)
        x = x + mem_cross_attention(rms_norm(x, norm_mem[l]), rms_norm(mem, norm_memkv[l]),
                                    mem_w_q[l], mem_w_kv[l], mem_w_o[l])
        x = x + swiglu(rms_norm(x, norm_ffn[l]), ffn_w_gu[l], ffn_w_down[l])
        if l == N_A - 1:
            k_blk, v_blk, k_mean = shared_kv(x, kv_norm, w_kv, cos, sin)
    return rms_norm(x, norm_final)
```

```python
import functools
import math

import jax
import jax.numpy as jnp
from jax import lax
from jax.experimental import pallas as pl
from jax.experimental.pallas import tpu as pltpu

D_MODEL = 1024
DEPTH = 2
CONV_W = 3
N_HEADS = 8
HEAD_DIM = D_MODEL // N_HEADS
BLOCK = 256
TOPK = 3
ROT_DIM = HEAD_DIM // 4
ROPE_THETA = 500000.0
MEM_LEN = 256
MEM_HEADS = 4
MEM_HEAD_DIM = D_MODEL // MEM_HEADS
D_FF = ((8 * D_MODEL + 3 * 256 - 1) // (3 * 256)) * 256
EPS = 1e-6

V7X_VMEM_BYTES = 64 * 1024 * 1024
V7X_LANES = 128
V7X_SUBLANES = 8
V7X_MXU_DIM = 256

TOKEN_TILE = 512
FF_CHUNK = 4 * V7X_MXU_DIM

_F32 = jnp.float32
_BF16 = jnp.bfloat16
_NT = (((1,), (1,)), ((), ()))


def _vmem_limit(block_bytes, temp_bytes):
    need = 2 * block_bytes + temp_bytes
    return int(min(need + need // 4, V7X_VMEM_BYTES - 8 * 1024 * 1024))


def _params(block_bytes, temp_bytes, ndim):
    return pltpu.CompilerParams(
        dimension_semantics=("arbitrary",) * ndim,
        vmem_limit_bytes=_vmem_limit(block_bytes, temp_bytes))


def _dot(a, b):
    return jnp.dot(a, b, preferred_element_type=_F32)


def _rms_unit(x):
    ms = jnp.mean(x * x, axis=-1, keepdims=True)
    return x * lax.rsqrt(ms + EPS)


def _memkv_kernel(mem_ref, g_ref, w_ref, kv_ref):
    h = (_rms_unit(mem_ref[0]) * g_ref[0]).astype(_BF16)
    kv_ref[0, 0] = _dot(h, w_ref[0]).astype(_BF16)


def _memkv(mem, norm_memkv, w_kv_bf16):
    bsz = mem.shape[0]
    blk = MEM_LEN * D_MODEL * 4 + D_MODEL * 2 * D_MODEL * 2 + MEM_LEN * 2 * D_MODEL * 2
    return pl.pallas_call(
        _memkv_kernel,
        grid=(DEPTH, bsz),
        in_specs=[
            pl.BlockSpec((1, MEM_LEN, D_MODEL), lambda l, b: (b, 0, 0)),
            pl.BlockSpec((1, 1, D_MODEL), lambda l, b: (l, 0, 0)),
            pl.BlockSpec((1, D_MODEL, 2 * D_MODEL), lambda l, b: (l, 0, 0)),
        ],
        out_specs=pl.BlockSpec((1, 1, MEM_LEN, 2 * D_MODEL), lambda l, b: (l, b, 0, 0)),
        out_shape=jax.ShapeDtypeStruct((DEPTH, bsz, MEM_LEN, 2 * D_MODEL), _BF16),
        compiler_params=_params(blk, 4 * MEM_LEN * 2 * D_MODEL * 4, 2),
        name="memkv",
    )(mem, norm_memkv.reshape(DEPTH, 1, D_MODEL), w_kv_bf16)


def _shift_rows(z, prev, k):
    zk = pltpu.roll(z, k, 0)
    pk = pltpu.roll(prev, k, 0)
    r = lax.broadcasted_iota(jnp.int32, prev.shape, 0)
    top = jnp.where(r < k, pk, zk[:V7X_SUBLANES])
    return jnp.concatenate([top, zk[V7X_SUBLANES:]], axis=0)


def _convmix_kernel(x_ref, g_ref, win_ref, cw_ref, wout_ref, o_ref, carry_ref, *, tiles_per_seq):
    d = D_MODEL

    @pl.when(pl.program_id(0) % tiles_per_seq == 0)
    def _():
        carry_ref[...] = jnp.zeros_like(carry_ref)

    x = x_ref[...]
    h = (_rms_unit(x) * g_ref[...]).astype(_BF16)
    b_gate = _dot(h, win_ref[:, 0:d])
    z = _dot(h, win_ref[:, d:2 * d]) * _dot(h, win_ref[:, 2 * d:3 * d])
    prev = carry_ref[...]
    carry_ref[...] = z[z.shape[0] - V7X_SUBLANES:, :]
    conv = (cw_ref[2:3, :] * z
            + cw_ref[1:2, :] * _shift_rows(z, prev, 1)
            + cw_ref[0:1, :] * _shift_rows(z, prev, 2))
    y = (b_gate * conv).astype(_BF16)
    o_ref[...] = x + _dot(y, wout_ref[...])


def _convmix(x2d, gain, w_in, w_conv, w_out, seq):
    n = x2d.shape[0]
    tm = TOKEN_TILE
    blk = 2 * tm * D_MODEL * 4 + 4 * D_MODEL * D_MODEL * 2
    return pl.pallas_call(
        functools.partial(_convmix_kernel, tiles_per_seq=seq // tm),
        grid=(n // tm,),
        in_specs=[
            pl.BlockSpec((tm, D_MODEL), lambda i: (i, 0)),
            pl.BlockSpec((1, D_MODEL), lambda i: (0, 0)),
            pl.BlockSpec((D_MODEL, 3 * D_MODEL), lambda i: (0, 0)),
            pl.BlockSpec((CONV_W, D_MODEL), lambda i: (0, 0)),
            pl.BlockSpec((D_MODEL, D_MODEL), lambda i: (0, 0)),
        ],
        out_specs=pl.BlockSpec((tm, D_MODEL), lambda i: (i, 0)),
        out_shape=jax.ShapeDtypeStruct((n, D_MODEL), _F32),
        scratch_shapes=[pltpu.VMEM((V7X_SUBLANES, D_MODEL), _F32)],
        compiler_params=_params(blk, 10 * tm * D_MODEL * 4, 1),
        name="convmix",
    )(x2d, gain.reshape(1, D_MODEL), w_in, w_conv, w_out)


def _memattn_kernel(*refs, has_pre):
    if has_pre:
        x_ref, a_ref, wpre_ref, g_ref, wq_ref, kv_ref, wo_ref, o_ref = refs
        x = x_ref[...] + _dot(a_ref[...], wpre_ref[...])
    else:
        x_ref, g_ref, wq_ref, kv_ref, wo_ref, o_ref = refs
        x = x_ref[...]
    h = (_rms_unit(x) * g_ref[...]).astype(_BF16)
    q = (_dot(h, wq_ref[...]) * (MEM_HEAD_DIM ** -0.5)).astype(_BF16)
    heads = []
    for hd in range(MEM_HEADS):
        c0 = hd * MEM_HEAD_DIM
        k_h = kv_ref[0, 0, :, c0:c0 + MEM_HEAD_DIM]
        v_h = kv_ref[0, 0, :, D_MODEL + c0:D_MODEL + c0 + MEM_HEAD_DIM]
        s = lax.dot_general(q[:, c0:c0 + MEM_HEAD_DIM], k_h, _NT, preferred_element_type=_F32)
        p = jnp.exp(s - jnp.max(s, axis=-1, keepdims=True))
        inv_l = 1.0 / jnp.sum(p, axis=-1, keepdims=True)
        heads.append((_dot(p.astype(_BF16), v_h) * inv_l).astype(_BF16))
    attn = jnp.concatenate(heads, axis=1)
    o_ref[...] = x + _dot(attn, wo_ref[...])


def _memattn(x2d, gain, w_q, kv, w_o, layer, seq, pre=None):
    n = x2d.shape[0]
    tm = TOKEN_TILE
    tiles_per_seq = seq // tm
    row_spec = pl.BlockSpec((tm, D_MODEL), lambda i: (i, 0))
    w_spec = pl.BlockSpec((D_MODEL, D_MODEL), lambda i: (0, 0))
    in_specs = [row_spec]
    args = [x2d]
    n_w = 2
    if pre is not None:
        in_specs += [row_spec, w_spec]
        args += list(pre)
        n_w = 3
    in_specs += [
        pl.BlockSpec((1, D_MODEL), lambda i: (0, 0)),
        w_spec,
        pl.BlockSpec((1, 1, MEM_LEN, 2 * D_MODEL), lambda i: (layer, i // tiles_per_seq, 0, 0)),
        w_spec,
    ]
    args += [gain.reshape(1, D_MODEL), w_q, kv, w_o]
    blk = 3 * tm * D_MODEL * 4 + n_w * D_MODEL * D_MODEL * 2 + MEM_LEN * 2 * D_MODEL * 2
    return pl.pallas_call(
        functools.partial(_memattn_kernel, has_pre=pre is not None),
        grid=(n // tm,),
        in_specs=in_specs,
        out_specs=row_spec,
        out_shape=jax.ShapeDtypeStruct((n, D_MODEL), _F32),
        compiler_params=_params(blk, 8 * tm * D_MODEL * 4, 1),
        name="memattn_l%d" % layer,
    )(*args)


def _ff_chunks():
    return [(c0, min(FF_CHUNK, D_FF - c0)) for c0 in range(0, D_FF, FF_CHUNK)]


def _swiglu_kernel(*refs, final):
    if final:
        x_ref, g_ref, wgu_ref, wd_ref, gf_ref, o_ref = refs
    else:
        x_ref, g_ref, wgu_ref, wd_ref, o_ref = refs
    x = x_ref[...]
    h = (_rms_unit(x) * g_ref[...]).astype(_BF16)
    acc = x
    for c0, cw in _ff_chunks():
        gate = _dot(h, wgu_ref[:, c0:c0 + cw])
        up = _dot(h, wgu_ref[:, D_FF + c0:D_FF + c0 + cw])
        act = (gate * (1.0 / (1.0 + jnp.exp(-gate))) * up).astype(_BF16)
        acc = acc + _dot(act, wd_ref[c0:c0 + cw, :])
    if final:
        acc = _rms_unit(acc) * gf_ref[...]
    o_ref[...] = acc


def _swiglu(x2d, gain, w_gu, w_down, final_gain=None):
    n = x2d.shape[0]
    tm = TOKEN_TILE
    g_spec = pl.BlockSpec((1, D_MODEL), lambda i: (0, 0))
    in_specs = [
        pl.BlockSpec((tm, D_MODEL), lambda i: (i, 0)),
        g_spec,
        pl.BlockSpec((D_MODEL, 2 * D_FF), lambda i: (0, 0), pipeline_mode=pl.Buffered(1)),
        pl.BlockSpec((D_FF, D_MODEL), lambda i: (0, 0), pipeline_mode=pl.Buffered(1)),
    ]
    args = [x2d, gain.reshape(1, D_MODEL), w_gu, w_down]
    if final_gain is not None:
        in_specs.append(g_spec)
        args.append(final_gain.reshape(1, D_MODEL))
    blk = 2 * tm * D_MODEL * 4 + 3 * D_MODEL * D_FF * 2 // 2
    return pl.pallas_call(
        functools.partial(_swiglu_kernel, final=final_gain is not None),
        grid=(n // tm,),
        in_specs=in_specs,
        out_specs=pl.BlockSpec((tm, D_MODEL), lambda i: (i, 0)),
        out_shape=jax.ShapeDtypeStruct((n, D_MODEL), _F32),
        compiler_params=_params(blk, 4 * tm * D_MODEL * 4 + 4 * tm * FF_CHUNK * 4, 1),
        name="swiglu_final" if final_gain is not None else "swiglu",
    )(*args)


def _rope(x, cos, sin_signed, first_half):
    d = x.shape[1]
    rot = jnp.where(first_half, pltpu.roll(x, d - ROT_DIM // 2, 1), pltpu.roll(x, ROT_DIM // 2, 1))
    return x * cos + rot * sin_signed


def _qkv_kernel(x_ref, pos_ref, invf_ref, gq_ref, gk_ref, wq_ref, wkv_ref,
                q_ref, k_ref, vt_ref, km_ref, *, tiles_per_seq):
    d = D_MODEL
    tm = x_ref.shape[0]
    blocks_per_tile = tm // BLOCK
    s_tile = pl.program_id(0) % tiles_per_seq

    @pl.when(s_tile == 0)
    def _():
        km_ref[...] = jnp.zeros_like(km_ref)

    xh = _rms_unit(x_ref[...])
    hq = (xh * gq_ref[...]).astype(_BF16)
    hk = (xh * gk_ref[...]).astype(_BF16)

    pos = pos_ref[0].astype(_F32)
    pos_rows = jnp.broadcast_to(pos, (V7X_LANES, tm)).T
    ang = pos_rows * invf_ref[...]
    lane = lax.broadcasted_iota(jnp.int32, (1, HEAD_DIM), 1)
    cos = jnp.cos(ang)
    sin_signed = jnp.where(lane < ROT_DIM // 2, -1.0, 1.0) * jnp.sin(ang)
    cos = jnp.concatenate([cos] * N_HEADS, axis=1)
    sin_signed = jnp.concatenate([sin_signed] * N_HEADS, axis=1)
    lane_d = lax.broadcasted_iota(jnp.int32, (1, d), 1)
    first_half = (lane_d % HEAD_DIM) < ROT_DIM // 2

    q = _dot(hq, wq_ref[...]) * (HEAD_DIM ** -0.5)
    q_ref[...] = _rope(q, cos, sin_signed, first_half).astype(_BF16)

    k = _rope(_dot(hk, wkv_ref[:, 0:d]), cos, sin_signed, first_half)
    k_ref[...] = k.astype(_BF16)
    v = _dot(hk, wkv_ref[:, d:2 * d])

    row16 = lax.broadcasted_iota(jnp.int32, km_ref.shape[1:], 0)
    for j in range(blocks_per_tile):
        r0 = j * BLOCK
        vt_ref[0, j] = v[r0:r0 + BLOCK, :].T.astype(_BF16)
        mean_row = jnp.sum(k[r0:r0 + BLOCK, :], axis=0, keepdims=True) * (1.0 / BLOCK)
        km_ref[0] = jnp.where(row16 == s_tile * blocks_per_tile + j, mean_row, km_ref[0])


def _qkv(x2d, positions, gain_q, gain_kv, w_q, w_kv, bsz, seq):
    n = x2d.shape[0]
    tm = TOKEN_TILE
    tiles_per_seq = seq // tm
    n_blocks = seq // BLOCK
    inv_freq = ROPE_THETA ** (-jnp.arange(0, ROT_DIM, 2, dtype=_F32) / ROT_DIM)
    invf_lane = jnp.concatenate(
        [inv_freq, inv_freq, jnp.zeros((HEAD_DIM - ROT_DIM,), _F32)]).reshape(1, HEAD_DIM)
    g_spec = pl.BlockSpec((1, D_MODEL), lambda i: (0, 0))
    row_spec = pl.BlockSpec((tm, D_MODEL), lambda i: (i, 0))
    blk = tm * D_MODEL * 4 + 3 * D_MODEL * D_MODEL * 2 + 3 * tm * D_MODEL * 2 + n_blocks * D_MODEL * 4
    return pl.pallas_call(
        functools.partial(_qkv_kernel, tiles_per_seq=tiles_per_seq),
        grid=(n // tm,),
        in_specs=[
            row_spec,
            pl.BlockSpec((1, 1, tm), lambda i: (i, 0, 0)),
            pl.BlockSpec((1, HEAD_DIM), lambda i: (0, 0)),
            g_spec, g_spec,
            pl.BlockSpec((D_MODEL, D_MODEL), lambda i: (0, 0)),
            pl.BlockSpec((D_MODEL, 2 * D_MODEL), lambda i: (0, 0)),
        ],
        out_specs=[
            row_spec,
            row_spec,
            pl.BlockSpec((1, tm // BLOCK, D_MODEL, BLOCK),
                         lambda i: (i // tiles_per_seq, i % tiles_per_seq, 0, 0)),
            pl.BlockSpec((1, n_blocks, D_MODEL), lambda i: (i // tiles_per_seq, 0, 0)),
        ],
        out_shape=[
            jax.ShapeDtypeStruct((n, D_MODEL), _BF16),
            jax.ShapeDtypeStruct((n, D_MODEL), _BF16),
            jax.ShapeDtypeStruct((bsz, n_blocks, D_MODEL, BLOCK), _BF16),
            jax.ShapeDtypeStruct((bsz, n_blocks, D_MODEL), _F32),
        ],
        compiler_params=_params(blk, 12 * tm * D_MODEL * 4, 1),
        name="qkv",
    )(x2d, positions.reshape(n // tm, 1, tm), invf_lane,
      gain_q.reshape(1, D_MODEL), gain_kv.reshape(1, D_MODEL), w_q, w_kv)


def _moba_kernel(q_ref, k_ref, vt_ref, km_ref, o_ref, bias_ref):
    t = pl.program_id(2)
    n_blocks = km_ref.shape[1]
    q = q_ref[...]

    km = km_ref[0]
    km_hi = km.astype(_BF16)
    km_lo = (km - km_hi.astype(_F32)).astype(_BF16)
    gate = (lax.dot_general(km_hi, q, _NT, preferred_element_type=_F32)
            + lax.dot_general(km_lo, q, _NT, preferred_element_type=_F32))
    blk = lax.broadcasted_iota(jnp.int32, gate.shape, 0)
    rank = jnp.zeros(gate.shape, _F32)
    for m in range(n_blocks):
        g_m = gate[m:m + 1, :]
        beats = jnp.where(g_m > gate, 1.0, jnp.where(g_m == gate, jnp.where(blk > m, 1.0, 0.0), 0.0))
        rank = rank + jnp.where(m < t, beats, 0.0)
    selected = jnp.where(blk < t, rank, float(TOPK)) < TOPK
    bias_ref[...] = jnp.where(selected, 0.0, -jnp.inf)

    k_own = k_ref[pl.ds(pl.multiple_of(t * BLOCK, BLOCK), BLOCK), :]
    s = lax.dot_general(k_own, q, _NT, preferred_element_type=_F32)
    kpos = lax.broadcasted_iota(jnp.int32, s.shape, 0)
    qpos = lax.broadcasted_iota(jnp.int32, s.shape, 1)
    s = jnp.where(kpos <= qpos, s, -jnp.inf)
    m0 = jnp.max(s, axis=0, keepdims=True)
    p = jnp.exp(s - m0)
    l0 = jnp.sum(p, axis=0, keepdims=True)
    acc0 = _dot(vt_ref[0, t], p.astype(_BF16))

    def past_block(n, carry):
        m_run, l_run, acc = carry
        k_n = k_ref[pl.ds(pl.multiple_of(n * BLOCK, BLOCK), BLOCK), :]
        s_n = lax.dot_general(k_n, q, _NT, preferred_element_type=_F32) + bias_ref[pl.ds(n, 1), :]
        m_new = jnp.maximum(m_run, jnp.max(s_n, axis=0, keepdims=True))
        alpha = jnp.exp(m_run - m_new)
        p_n = jnp.exp(s_n - m_new)
        l_new = alpha * l_run + jnp.sum(p_n, axis=0, keepdims=True)
        acc_new = alpha * acc + _dot(vt_ref[0, n], p_n.astype(_BF16))
        return m_new, l_new, acc_new

    _, l_fin, acc = lax.fori_loop(0, t, past_block, (m0, l0, acc0))
    o_ref[...] = (acc * (1.0 / l_fin)).T.astype(_BF16)


def _moba(q, k, vt, k_mean, bsz, seq):
    n = q.shape[0]
    n_blocks = seq // BLOCK
    blk = (2 * BLOCK * HEAD_DIM * 2 + 2 * seq * HEAD_DIM * 2 + n_blocks * HEAD_DIM * 4)
    return pl.pallas_call(
        _moba_kernel,
        grid=(bsz, N_HEADS, n_blocks),
        in_specs=[
            pl.BlockSpec((BLOCK, HEAD_DIM), lambda b, h, t: (b * n_blocks + t, h)),
            pl.BlockSpec((seq, HEAD_DIM), lambda b, h, t: (b, h)),
            pl.BlockSpec((1, n_blocks, HEAD_DIM, BLOCK), lambda b, h, t: (b, 0, h, 0)),
            pl.BlockSpec((1, n_blocks, HEAD_DIM), lambda b, h, t: (b, 0, h)),
        ],
        out_specs=pl.BlockSpec((BLOCK, HEAD_DIM), lambda b, h, t: (b * n_blocks + t, h)),
        out_shape=jax.ShapeDtypeStruct((n, D_MODEL), _BF16),
        scratch_shapes=[pltpu.VMEM((n_blocks, BLOCK), _F32)],
        compiler_params=_params(blk, 16 * BLOCK * BLOCK * 4, 3),
        name="moba",
    )(q, k, vt, k_mean)


def kernel(x, mem, positions, norm_mix, norm_mem, norm_memkv, norm_ffn, norm_final,
           conv_w_in, conv_w, conv_w_out, kv_norm, w_kv, moba_w_q, moba_w_o,
           mem_w_q, mem_w_kv, mem_w_o, ffn_w_gu, ffn_w_down):
    bsz, seq, d = x.shape
    assert d == D_MODEL and seq % TOKEN_TILE == 0 and TOKEN_TILE % BLOCK == 0
    assert mem.shape == (bsz, MEM_LEN, D_MODEL)
    assert norm_mix.shape[0] == DEPTH == 2 and conv_w_in.shape[0] == 1 and moba_w_q.shape[0] == 1
    bf = lambda w: w.astype(_BF16)

    x2d = x.reshape(bsz * seq, d)
    kv_mem = _memkv(mem, norm_memkv, bf(mem_w_kv))

    x2d = _convmix(x2d, norm_mix[0], bf(conv_w_in[0]), conv_w[0], bf(conv_w_out[0]), seq)
    x2d = _memattn(x2d, norm_mem[0], bf(mem_w_q[0]), kv_mem, bf(mem_w_o[0]), 0, seq)
    x2d = _swiglu(x2d, norm_ffn[0], bf(ffn_w_gu[0]), bf(ffn_w_down[0]))

    q, k, vt, k_mean = _qkv(x2d, positions, norm_mix[1], kv_norm, bf(moba_w_q[0]), bf(w_kv), bsz, seq)
    attn = _moba(q, k, vt, k_mean, bsz, seq)

    x2d = _memattn(x2d, norm_mem[1], bf(mem_w_q[1]), kv_mem, bf(mem_w_o[1]), 1, seq,
                   pre=(attn, bf(moba_w_o[0])))
    x2d = _swiglu(x2d, norm_ffn[1], bf(ffn_w_gu[1]), bf(ffn_w_down[1]), final_gain=norm_final)
    return x2d.reshape(bsz, seq, d)
```

```python
import functools
import math

import jax
import jax.numpy as jnp
from jax import lax
from jax.experimental import pallas as pl
from jax.experimental.pallas import tpu as pltpu

D_MODEL = 1024
DEPTH = 2
CONV_W = 3
N_HEADS = 8
HEAD_DIM = D_MODEL // N_HEADS
BLOCK = 256
TOPK = 3
ROT_DIM = HEAD_DIM // 4
ROPE_THETA = 500000.0
MEM_LEN = 256
MEM_HEADS = 4
MEM_HEAD_DIM = D_MODEL // MEM_HEADS
D_FF = ((8 * D_MODEL + 3 * 256 - 1) // (3 * 256)) * 256
EPS = 1e-6

V7X_VMEM_BYTES = 64 * 1024 * 1024
V7X_LANES = 128
V7X_SUBLANES = 8
V7X_MXU_DIM = 256

TOKEN_TILE = 512
FF_CHUNK = 4 * V7X_MXU_DIM

_F32 = jnp.float32
_BF16 = jnp.bfloat16
_NT = (((1,), (1,)), ((), ()))


def _vmem_limit(block_bytes, temp_bytes):
    need = 2 * block_bytes + temp_bytes
    return int(min(need + need // 4, V7X_VMEM_BYTES - 8 * 1024 * 1024))


def _params(block_bytes, temp_bytes, ndim):
    return pltpu.CompilerParams(
        dimension_semantics=("arbitrary",) * ndim,
        vmem_limit_bytes=_vmem_limit(block_bytes, temp_bytes))


def _dot(a, b):
    return jnp.dot(a, b, preferred_element_type=_F32)


def _rms_unit(x):
    ms = jnp.mean(x * x, axis=-1, keepdims=True)
    return x * lax.rsqrt(ms + EPS)


def _memkv_kernel(mem_ref, g_ref, w_ref, kv_ref):
    h = (_rms_unit(mem_ref[0]) * g_ref[0]).astype(_BF16)
    kv_ref[0, 0] = _dot(h, w_ref[0]).astype(_BF16)


def _memkv(mem, norm_memkv, w_kv_bf16):
    bsz = mem.shape[0]
    blk = MEM_LEN * D_MODEL * 4 + D_MODEL * 2 * D_MODEL * 2 + MEM_LEN * 2 * D_MODEL * 2
    return pl.pallas_call(
        _memkv_kernel,
        grid=(DEPTH, bsz),
        in_specs=[
            pl.BlockSpec((1, MEM_LEN, D_MODEL), lambda l, b: (b, 0, 0)),
            pl.BlockSpec((1, 1, D_MODEL), lambda l, b: (l, 0, 0)),
            pl.BlockSpec((1, D_MODEL, 2 * D_MODEL), lambda l, b: (l, 0, 0)),
        ],
        out_specs=pl.BlockSpec((1, 1, MEM_LEN, 2 * D_MODEL), lambda l, b: (l, b, 0, 0)),
        out_shape=jax.ShapeDtypeStruct((DEPTH, bsz, MEM_LEN, 2 * D_MODEL), _BF16),
        compiler_params=_params(blk, 4 * MEM_LEN * 2 * D_MODEL * 4, 2),
        name="memkv",
    )(mem, norm_memkv.reshape(DEPTH, 1, D_MODEL), w_kv_bf16)


def _shift_rows(z, prev, k):
    zk = pltpu.roll(z, k, 0)
    pk = pltpu.roll(prev, k, 0)
    r = lax.broadcasted_iota(jnp.int32, prev.shape, 0)
    top = jnp.where(r < k, pk, zk[:V7X_SUBLANES])
    return jnp.concatenate([top, zk[V7X_SUBLANES:]], axis=0)


def _convmix_kernel(x_ref, g_ref, win_ref, cw_ref, wout_ref, o_ref, carry_ref, *, tiles_per_seq):
    d = D_MODEL

    @pl.when(pl.program_id(0) % tiles_per_seq == 0)
    def _():
        carry_ref[...] = jnp.zeros_like(carry_ref)

    x = x_ref[...]
    h = (_rms_unit(x) * g_ref[...]).astype(_BF16)
    b_gate = _dot(h, win_ref[:, 0:d])
    z = _dot(h, win_ref[:, d:2 * d]) * _dot(h, win_ref[:, 2 * d:3 * d])
    prev = carry_ref[...]
    carry_ref[...] = z[z.shape[0] - V7X_SUBLANES:, :]
    conv = (cw_ref[2:3, :] * z
            + cw_ref[1:2, :] * _shift_rows(z, prev, 1)
            + cw_ref[0:1, :] * _shift_rows(z, prev, 2))
    y = (b_gate * conv).astype(_BF16)
    o_ref[...] = x + _dot(y, wout_ref[...])


def _convmix(x2d, gain, w_in, w_conv, w_out, seq):
    n = x2d.shape[0]
    tm = TOKEN_TILE
    blk = 2 * tm * D_MODEL * 4 + 4 * D_MODEL * D_MODEL * 2
    return pl.pallas_call(
        functools.partial(_convmix_kernel, tiles_per_seq=seq // tm),
        grid=(n // tm,),
        in_specs=[
            pl.BlockSpec((tm, D_MODEL), lambda i: (i, 0)),
            pl.BlockSpec((1, D_MODEL), lambda i: (0, 0)),
            pl.BlockSpec((D_MODEL, 3 * D_MODEL), lambda i: (0, 0)),
            pl.BlockSpec((CONV_W, D_MODEL), lambda i: (0, 0)),
            pl.BlockSpec((D_MODEL, D_MODEL), lambda i: (0, 0)),
        ],
        out_specs=pl.BlockSpec((tm, D_MODEL), lambda i: (i, 0)),
        out_shape=jax.ShapeDtypeStruct((n, D_MODEL), _F32),
        scratch_shapes=[pltpu.VMEM((V7X_SUBLANES, D_MODEL), _F32)],
        compiler_params=_params(blk, 10 * tm * D_MODEL * 4, 1),
        name="convmix",
    )(x2d, gain.reshape(1, D_MODEL), w_in, w_conv, w_out)


def _memattn_kernel(*refs, has_pre):
    if has_pre:
        x_ref, a_ref, wpre_ref, g_ref, wq_ref, kv_ref, wo_ref, o_ref = refs
        x = x_ref[...] + _dot(a_ref[...], wpre_ref[...])
    else:
        x_ref, g_ref, wq_ref, kv_ref, wo_ref, o_ref = refs
        x = x_ref[...]
    h = (_rms_unit(x) * g_ref[...]).astype(_BF16)
    q = (_dot(h, wq_ref[...]) * (MEM_HEAD_DIM ** -0.5)).astype(_BF16)
    heads = []
    for hd in range(MEM_HEADS):
        c0 = hd * MEM_HEAD_DIM
        k_h = kv_ref[0, 0, :, c0:c0 + MEM_HEAD_DIM]
        v_h = kv_ref[0, 0, :, D_MODEL + c0:D_MODEL + c0 + MEM_HEAD_DIM]
        s = lax.dot_general(q[:, c0:c0 + MEM_HEAD_DIM], k_h, _NT, preferred_element_type=_F32)
        p = jnp.exp(s - jnp.max(s, axis=-1, keepdims=True))
        inv_l = 1.0 / jnp.sum(p, axis=-1, keepdims=True)
        heads.append((_dot(p.astype(_BF16), v_h) * inv_l).astype(_BF16))
    attn = jnp.concatenate(heads, axis=1)
    o_ref[...] = x + _dot(attn, wo_ref[...])


def _memattn(x2d, gain, w_q, kv, w_o, layer, seq, pre=None):
    n = x2d.shape[0]
    tm = TOKEN_TILE
    tiles_per_seq = seq // tm
    row_spec = pl.BlockSpec((tm, D_MODEL), lambda i: (i, 0))
    w_spec = pl.BlockSpec((D_MODEL, D_MODEL), lambda i: (0, 0))
    in_specs = [row_spec]
    args = [x2d]
    n_w = 2
    if pre is not None:
        in_specs += [row_spec, w_spec]
        args += list(pre)
        n_w = 3
    in_specs += [
        pl.BlockSpec((1, D_MODEL), lambda i: (0, 0)),
        w_spec,
        pl.BlockSpec((1, 1, MEM_LEN, 2 * D_MODEL), lambda i: (layer, i // tiles_per_seq, 0, 0)),
        w_spec,
    ]
    args += [gain.reshape(1, D_MODEL), w_q, kv, w_o]
    blk = 3 * tm * D_MODEL * 4 + n_w * D_MODEL * D_MODEL * 2 + MEM_LEN * 2 * D_MODEL * 2
    return pl.pallas_call(
        functools.partial(_memattn_kernel, has_pre=pre is not None),
        grid=(n // tm,),
        in_specs=in_specs,
        out_specs=row_spec,
        out_shape=jax.ShapeDtypeStruct((n, D_MODEL), _F32),
        compiler_params=_params(blk, 8 * tm * D_MODEL * 4, 1),
        name="memattn_l%d" % layer,
    )(*args)


def _ff_chunks():
    return [(c0, min(FF_CHUNK, D_FF - c0)) for c0 in range(0, D_FF, FF_CHUNK)]


def _swiglu_kernel(*refs, final):
    if final:
        x_ref, g_ref, wgu_ref, wd_ref, gf_ref, o_ref = refs
    else:
        x_ref, g_ref, wgu_ref, wd_ref, o_ref = refs
    x = x_ref[...]
    h = (_rms_unit(x) * g_ref[...]).astype(_BF16)
    acc = x
    for c0, cw in _ff_chunks():
        gate = _dot(h, wgu_ref[:, c0:c0 + cw])
        up = _dot(h, wgu_ref[:, D_FF + c0:D_FF + c0 + cw])
        act = (gate * (1.0 / (1.0 + jnp.exp(-gate))) * up).astype(_BF16)
        acc = acc + _dot(act, wd_ref[c0:c0 + cw, :])
    if final:
        acc = _rms_unit(acc) * gf_ref[...]
    o_ref[...] = acc


def _swiglu(x2d, gain, w_gu, w_down, final_gain=None):
    n = x2d.shape[0]
    tm = TOKEN_TILE
    g_spec = pl.BlockSpec((1, D_MODEL), lambda i: (0, 0))
    in_specs = [
        pl.BlockSpec((tm, D_MODEL), lambda i: (i, 0)),
        g_spec,
        pl.BlockSpec((D_MODEL, 2 * D_FF), lambda i: (0, 0), pipeline_mode=pl.Buffered(1)),
        pl.BlockSpec((D_FF, D_MODEL), lambda i: (0, 0), pipeline_mode=pl.Buffered(1)),
    ]
    args = [x2d, gain.reshape(1, D_MODEL), w_gu, w_down]
    if final_gain is not None:
        in_specs.append(g_spec)
        args.append(final_gain.reshape(1, D_MODEL))
    blk = 2 * tm * D_MODEL * 4 + 3 * D_MODEL * D_FF * 2 // 2
    return pl.pallas_call(
        functools.partial(_swiglu_kernel, final=final_gain is not None),
        grid=(n // tm,),
        in_specs=in_specs,
        out_specs=pl.BlockSpec((tm, D_MODEL), lambda i: (i, 0)),
        out_shape=jax.ShapeDtypeStruct((n, D_MODEL), _F32),
        compiler_params=_params(blk, 4 * tm * D_MODEL * 4 + 4 * tm * FF_CHUNK * 4, 1),
        name="swiglu_final" if final_gain is not None else "swiglu",
    )(*args)


def _rope(x, cos, sin_signed, first_half):
    d = x.shape[1]
    rot = jnp.where(first_half, pltpu.roll(x, d - ROT_DIM // 2, 1), pltpu.roll(x, ROT_DIM // 2, 1))
    return x * cos + rot * sin_signed


def _qkv_kernel(x_ref, pos_ref, invf_ref, gq_ref, gk_ref, wq_ref, wkv_ref,
                q_ref, k_ref, vt_ref, km_ref, *, tiles_per_seq):
    d = D_MODEL
    tm = x_ref.shape[0]
    blocks_per_tile = tm // BLOCK
    s_tile = pl.program_id(0) % tiles_per_seq

    @pl.when(s_tile == 0)
    def _():
        km_ref[...] = jnp.zeros_like(km_ref)

    xh = _rms_unit(x_ref[...])
    hq = (xh * gq_ref[...]).astype(_BF16)
    hk = (xh * gk_ref[...]).astype(_BF16)

    pos = pos_ref[0].astype(_F32)
    pos_rows = jnp.broadcast_to(pos, (V7X_LANES, tm)).T
    ang = pos_rows * invf_ref[...]
    lane = lax.broadcasted_iota(jnp.int32, (1, HEAD_DIM), 1)
    cos = jnp.cos(ang)
    sin_signed = jnp.where(lane < ROT_DIM // 2, -1.0, 1.0) * jnp.sin(ang)
    cos = jnp.concatenate([cos] * N_HEADS, axis=1)
    sin_signed = jnp.concatenate([sin_signed] * N_HEADS, axis=1)
    lane_d = lax.broadcasted_iota(jnp.int32, (1, d), 1)
    first_half = (lane_d % HEAD_DIM) < ROT_DIM // 2

    q = _dot(hq, wq_ref[...]) * (HEAD_DIM ** -0.5)
    q_ref[...] = _rope(q, cos, sin_signed, first_half).astype(_BF16)

    k = _rope(_dot(hk, wkv_ref[:, 0:d]), cos, sin_signed, first_half)
    k_ref[...] = k.astype(_BF16)
    v = _dot(hk, wkv_ref[:, d:2 * d])

    row16 = lax.broadcasted_iota(jnp.int32, km_ref.shape[1:], 0)
    for j in range(blocks_per_tile):
        r0 = j * BLOCK
        vt_ref[0, j] = v[r0:r0 + BLOCK, :].T.astype(_BF16)
        mean_row = jnp.sum(k[r0:r0 + BLOCK, :], axis=0, keepdims=True) * (1.0 / BLOCK)
        km_ref[0] = jnp.where(row16 == s_tile * blocks_per_tile + j, mean_row, km_ref[0])


def _qkv(x2d, positions, gain_q, gain_kv, w_q, w_kv, bsz, seq):
    n = x2d.shape[0]
    tm = TOKEN_TILE
    tiles_per_seq = seq // tm
    n_blocks = seq // BLOCK
    inv_freq = ROPE_THETA ** (-jnp.arange(0, ROT_DIM, 2, dtype=_F32) / ROT_DIM)
    invf_lane = jnp.concatenate(
        [inv_freq, inv_freq, jnp.zeros((HEAD_DIM - ROT_DIM,), _F32)]).reshape(1, HEAD_DIM)
    g_spec = pl.BlockSpec((1, D_MODEL), lambda i: (0, 0))
    row_spec = pl.BlockSpec((tm, D_MODEL), lambda i: (i, 0))
    blk = tm * D_MODEL * 4 + 3 * D_MODEL * D_MODEL * 2 + 3 * tm * D_MODEL * 2 + n_blocks * D_MODEL * 4
    return pl.pallas_call(
        functools.partial(_qkv_kernel, tiles_per_seq=tiles_per_seq),
        grid=(n // tm,),
        in_specs=[
            row_spec,
            pl.BlockSpec((1, 1, tm), lambda i: (i, 0, 0)),
            pl.BlockSpec((1, HEAD_DIM), lambda i: (0, 0)),
            g_spec, g_spec,
            pl.BlockSpec((D_MODEL, D_MODEL), lambda i: (0, 0)),
            pl.BlockSpec((D_MODEL, 2 * D_MODEL), lambda i: (0, 0)),
        ],
        out_specs=[
            row_spec,
            row_spec,
            pl.BlockSpec((1, tm // BLOCK, D_MODEL, BLOCK),
                         lambda i: (i // tiles_per_seq, i % tiles_per_seq, 0, 0)),
            pl.BlockSpec((1, n_blocks, D_MODEL), lambda i: (i // tiles_per_seq, 0, 0)),
        ],
        out_shape=[
            jax.ShapeDtypeStruct((n, D_MODEL), _BF16),
            jax.ShapeDtypeStruct((n, D_MODEL), _BF16),
            jax.ShapeDtypeStruct((bsz, n_blocks, D_MODEL, BLOCK), _BF16),
            jax.ShapeDtypeStruct((bsz, n_blocks, D_MODEL), _F32),
        ],
        compiler_params=_params(blk, 12 * tm * D_MODEL * 4, 1),
        name="qkv",
    )(x2d, positions.reshape(n // tm, 1, tm), invf_lane,
      gain_q.reshape(1, D_MODEL), gain_kv.reshape(1, D_MODEL), w_q, w_kv)


def _moba_kernel(q_ref, k_ref, vt_ref, km_ref, o_ref, bias_ref, m_ref, l_ref, acc_ref):
    t = pl.program_id(1)
    n_blocks = km_ref.shape[1]
    own_rows = pl.ds(pl.multiple_of(t * BLOCK, BLOCK), BLOCK)

    head_cols = [slice(h * HEAD_DIM, (h + 1) * HEAD_DIM) for h in range(N_HEADS)]

    gates = []
    for c in head_cols:
        km = km_ref[0, :, c]
        km_hi = km.astype(_BF16)
        km_lo = (km - km_hi.astype(_F32)).astype(_BF16)
        gates.append(lax.dot_general(km_hi, q_ref[:, c], _NT, preferred_element_type=_F32)
                     + lax.dot_general(km_lo, q_ref[:, c], _NT, preferred_element_type=_F32))
    own_scores = [lax.dot_general(k_ref[own_rows, c], q_ref[:, c], _NT, preferred_element_type=_F32)
                  for c in head_cols]

    blk = lax.broadcasted_iota(jnp.int32, (n_blocks, BLOCK), 0)
    for h in range(N_HEADS):
        gate = gates[h]
        rank = jnp.zeros(gate.shape, _F32)
        for m in range(n_blocks):
            g_m = gate[m:m + 1, :]
            beats = jnp.where(g_m > gate, 1.0, jnp.where(g_m == gate, jnp.where(blk > m, 1.0, 0.0), 0.0))
            rank = rank + jnp.where(m < t, beats, 0.0)
        selected = jnp.where(blk < t, rank, float(TOPK)) < TOPK
        bias_ref[h] = jnp.where(selected, 0.0, -jnp.inf)

    kpos = lax.broadcasted_iota(jnp.int32, (BLOCK, BLOCK), 0)
    qpos = lax.broadcasted_iota(jnp.int32, (BLOCK, BLOCK), 1)
    own_probs = []
    for h in range(N_HEADS):
        s = jnp.where(kpos <= qpos, own_scores[h], -jnp.inf)
        m0 = jnp.max(s, axis=0, keepdims=True)
        p = jnp.exp(s - m0)
        m_ref[h:h + 1, :] = m0
        l_ref[h:h + 1, :] = jnp.sum(p, axis=0, keepdims=True)
        own_probs.append(p.astype(_BF16))
    for h in range(N_HEADS):
        acc_ref[h] = _dot(vt_ref[0, t, head_cols[h], :], own_probs[h])

    def past_block(n, carry):
        rows = pl.ds(pl.multiple_of(n * BLOCK, BLOCK), BLOCK)
        scores = [lax.dot_general(k_ref[rows, c], q_ref[:, c], _NT, preferred_element_type=_F32)
                  for c in head_cols]
        probs, alphas = [], []
        for h in range(N_HEADS):
            s_n = scores[h] + bias_ref[h, pl.ds(n, 1), :]
            m_run = m_ref[h:h + 1, :]
            m_new = jnp.maximum(m_run, jnp.max(s_n, axis=0, keepdims=True))
            alpha = jnp.exp(m_run - m_new)
            p_n = jnp.exp(s_n - m_new)
            m_ref[h:h + 1, :] = m_new
            l_ref[h:h + 1, :] = alpha * l_ref[h:h + 1, :] + jnp.sum(p_n, axis=0, keepdims=True)
            probs.append(p_n.astype(_BF16))
            alphas.append(alpha)
        for h in range(N_HEADS):
            acc_ref[h] = alphas[h] * acc_ref[h] + _dot(vt_ref[0, n, head_cols[h], :], probs[h])
        return carry

    lax.fori_loop(0, t, past_block, 0)
    for h in range(N_HEADS):
        out_h = acc_ref[h] * (1.0 / l_ref[h:h + 1, :])
        o_ref[:, h * HEAD_DIM:(h + 1) * HEAD_DIM] = out_h.T.astype(_BF16)


def _moba(q, k, vt, k_mean, bsz, seq):
    n = q.shape[0]
    n_blocks = seq // BLOCK
    blk = 2 * BLOCK * D_MODEL * 2 + 2 * seq * D_MODEL * 2 + n_blocks * D_MODEL * 4
    scratch = N_HEADS * (n_blocks + HEAD_DIM + 2) * BLOCK * 4
    return pl.pallas_call(
        _moba_kernel,
        grid=(bsz, n_blocks),
        in_specs=[
            pl.BlockSpec((BLOCK, D_MODEL), lambda b, t: (b * n_blocks + t, 0)),
            pl.BlockSpec((seq, D_MODEL), lambda b, t: (b, 0)),
            pl.BlockSpec((1, n_blocks, D_MODEL, BLOCK), lambda b, t: (b, 0, 0, 0)),
            pl.BlockSpec((1, n_blocks, D_MODEL), lambda b, t: (b, 0, 0)),
        ],
        out_specs=pl.BlockSpec((BLOCK, D_MODEL), lambda b, t: (b * n_blocks + t, 0)),
        out_shape=jax.ShapeDtypeStruct((n, D_MODEL), _BF16),
        scratch_shapes=[
            pltpu.VMEM((N_HEADS, n_blocks, BLOCK), _F32),
            pltpu.VMEM((N_HEADS, BLOCK), _F32),
            pltpu.VMEM((N_HEADS, BLOCK), _F32),
            pltpu.VMEM((N_HEADS, HEAD_DIM, BLOCK), _F32),
        ],
        compiler_params=_params(blk, scratch + 24 * BLOCK * BLOCK * 4, 2),
        name="moba",
    )(q, k, vt, k_mean)


def kernel(x, mem, positions, norm_mix, norm_mem, norm_memkv, norm_ffn, norm_final,
           conv_w_in, conv_w, conv_w_out, kv_norm, w_kv, moba_w_q, moba_w_o,
           mem_w_q, mem_w_kv, mem_w_o, ffn_w_gu, ffn_w_down):
    bsz, seq, d = x.shape
    assert d == D_MODEL and seq % TOKEN_TILE == 0 and TOKEN_TILE % BLOCK == 0
    assert mem.shape == (bsz, MEM_LEN, D_MODEL)
    assert norm_mix.shape[0] == DEPTH == 2 and conv_w_in.shape[0] == 1 and moba_w_q.shape[0] == 1
    bf = lambda w: w.astype(_BF16)

    x2d = x.reshape(bsz * seq, d)
    kv_mem = _memkv(mem, norm_memkv, bf(mem_w_kv))

    x2d = _convmix(x2d, norm_mix[0], bf(conv_w_in[0]), conv_w[0], bf(conv_w_out[0]), seq)
    x2d = _memattn(x2d, norm_mem[0], bf(mem_w_q[0]), kv_mem, bf(mem_w_o[0]), 0, seq)
    x2d = _swiglu(x2d, norm_ffn[0], bf(ffn_w_gu[0]), bf(ffn_w_down[0]))

    q, k, vt, k_mean = _qkv(x2d, positions, norm_mix[1], kv_norm, bf(moba_w_q[0]), bf(w_kv), bsz, seq)
    attn = _moba(q, k, vt, k_mean, bsz, seq)

    x2d = _memattn(x2d, norm_mem[1], bf(mem_w_q[1]), kv_mem, bf(mem_w_o[1]), 1, seq,
                   pre=(attn, bf(moba_w_o[0])))
    x2d = _swiglu(x2d, norm_ffn[1], bf(ffn_w_gu[1]), bf(ffn_w_down[1]), final_gain=norm_final)
    return x2d.reshape(bsz, seq, d)
```

```python
import functools
import math

import jax
import jax.numpy as jnp
from jax import lax
from jax.experimental import pallas as pl
from jax.experimental.pallas import tpu as pltpu

D_MODEL = 1024
DEPTH = 2
CONV_W = 3
N_HEADS = 8
HEAD_DIM = D_MODEL // N_HEADS
BLOCK = 256
TOPK = 3
ROT_DIM = HEAD_DIM // 4
ROPE_THETA = 500000.0
MEM_LEN = 256
MEM_HEADS = 4
MEM_HEAD_DIM = D_MODEL // MEM_HEADS
D_FF = ((8 * D_MODEL + 3 * 256 - 1) // (3 * 256)) * 256
EPS = 1e-6

V7X_VMEM_BYTES = 64 * 1024 * 1024
V7X_LANES = 128
V7X_SUBLANES = 8
V7X_MXU_DIM = 256
_BF16_SUBLANES = 2 * V7X_SUBLANES

TOKEN_TILE = 512
FF_CHUNK = 4 * V7X_MXU_DIM

_F32 = jnp.float32
_BF16 = jnp.bfloat16
_NT = (((1,), (1,)), ((), ()))


def _vmem_limit(block_bytes, temp_bytes):
    need = 2 * block_bytes + temp_bytes
    return int(min(need + need // 4, V7X_VMEM_BYTES - 8 * 1024 * 1024))


def _params(block_bytes, temp_bytes, ndim):
    return pltpu.CompilerParams(
        dimension_semantics=("arbitrary",) * ndim,
        vmem_limit_bytes=_vmem_limit(block_bytes, temp_bytes))


def _dot(a, b):
    return jnp.dot(a, b, preferred_element_type=_F32)


def _rms_unit(x):
    ms = jnp.mean(x * x, axis=-1, keepdims=True)
    return x * lax.rsqrt(ms + EPS)


def _memkv_kernel(mem_ref, g_ref, w_ref, kv_ref):
    h = (_rms_unit(mem_ref[0]) * g_ref[0]).astype(_BF16)
    kv_ref[0, 0] = _dot(h, w_ref[0]).astype(_BF16)


def _memkv(mem, norm_memkv, w_kv_bf16):
    bsz = mem.shape[0]
    blk = MEM_LEN * D_MODEL * 4 + D_MODEL * 2 * D_MODEL * 2 + MEM_LEN * 2 * D_MODEL * 2
    return pl.pallas_call(
        _memkv_kernel,
        grid=(DEPTH, bsz),
        in_specs=[
            pl.BlockSpec((1, MEM_LEN, D_MODEL), lambda l, b: (b, 0, 0)),
            pl.BlockSpec((1, 1, D_MODEL), lambda l, b: (l, 0, 0)),
            pl.BlockSpec((1, D_MODEL, 2 * D_MODEL), lambda l, b: (l, 0, 0)),
        ],
        out_specs=pl.BlockSpec((1, 1, MEM_LEN, 2 * D_MODEL), lambda l, b: (l, b, 0, 0)),
        out_shape=jax.ShapeDtypeStruct((DEPTH, bsz, MEM_LEN, 2 * D_MODEL), _BF16),
        compiler_params=_params(blk, 4 * MEM_LEN * 2 * D_MODEL * 4, 2),
        name="memkv",
    )(mem, norm_memkv.reshape(DEPTH, 1, D_MODEL), w_kv_bf16)


def _shift_rows(z, prev, k):
    zk = pltpu.roll(z, k, 0)
    pk = pltpu.roll(prev, k, 0)
    r = lax.broadcasted_iota(jnp.int32, prev.shape, 0)
    top = jnp.where(r < k, pk, zk[:V7X_SUBLANES])
    return jnp.concatenate([top, zk[V7X_SUBLANES:]], axis=0)


def _convmix_kernel(x_ref, g_ref, win_ref, cw_ref, wout_ref, o_ref, carry_ref, *, tiles_per_seq):
    d = D_MODEL

    @pl.when(pl.program_id(0) % tiles_per_seq == 0)
    def _():
        carry_ref[...] = jnp.zeros_like(carry_ref)

    x = x_ref[...]
    h = (_rms_unit(x) * g_ref[...]).astype(_BF16)
    b_gate = _dot(h, win_ref[:, 0:d])
    z = _dot(h, win_ref[:, d:2 * d]) * _dot(h, win_ref[:, 2 * d:3 * d])
    prev = carry_ref[...]
    carry_ref[...] = z[z.shape[0] - V7X_SUBLANES:, :]
    conv = (cw_ref[2:3, :] * z
            + cw_ref[1:2, :] * _shift_rows(z, prev, 1)
            + cw_ref[0:1, :] * _shift_rows(z, prev, 2))
    y = (b_gate * conv).astype(_BF16)
    o_ref[...] = x + _dot(y, wout_ref[...])


def _convmix(x2d, gain, w_in, w_conv, w_out, seq):
    n = x2d.shape[0]
    tm = TOKEN_TILE
    blk = 2 * tm * D_MODEL * 4 + 4 * D_MODEL * D_MODEL * 2
    return pl.pallas_call(
        functools.partial(_convmix_kernel, tiles_per_seq=seq // tm),
        grid=(n // tm,),
        in_specs=[
            pl.BlockSpec((tm, D_MODEL), lambda i: (i, 0)),
            pl.BlockSpec((1, D_MODEL), lambda i: (0, 0)),
            pl.BlockSpec((D_MODEL, 3 * D_MODEL), lambda i: (0, 0)),
            pl.BlockSpec((CONV_W, D_MODEL), lambda i: (0, 0)),
            pl.BlockSpec((D_MODEL, D_MODEL), lambda i: (0, 0)),
        ],
        out_specs=pl.BlockSpec((tm, D_MODEL), lambda i: (i, 0)),
        out_shape=jax.ShapeDtypeStruct((n, D_MODEL), _F32),
        scratch_shapes=[pltpu.VMEM((V7X_SUBLANES, D_MODEL), _F32)],
        compiler_params=_params(blk, 10 * tm * D_MODEL * 4, 1),
        name="convmix",
    )(x2d, gain.reshape(1, D_MODEL), w_in, w_conv, w_out)


def _memattn_kernel(*refs, has_pre):
    if has_pre:
        x_ref, a_ref, wpre_ref, g_ref, wq_ref, kv_ref, wo_ref, o_ref = refs
        x = x_ref[...] + _dot(a_ref[...], wpre_ref[...])
    else:
        x_ref, g_ref, wq_ref, kv_ref, wo_ref, o_ref = refs
        x = x_ref[...]
    h = (_rms_unit(x) * g_ref[...]).astype(_BF16)
    q = (_dot(h, wq_ref[...]) * (MEM_HEAD_DIM ** -0.5)).astype(_BF16)
    heads = []
    for hd in range(MEM_HEADS):
        c0 = hd * MEM_HEAD_DIM
        k_h = kv_ref[0, 0, :, c0:c0 + MEM_HEAD_DIM]
        v_h = kv_ref[0, 0, :, D_MODEL + c0:D_MODEL + c0 + MEM_HEAD_DIM]
        s = lax.dot_general(q[:, c0:c0 + MEM_HEAD_DIM], k_h, _NT, preferred_element_type=_F32)
        p = jnp.exp(s - jnp.max(s, axis=-1, keepdims=True))
        inv_l = 1.0 / jnp.sum(p, axis=-1, keepdims=True)
        heads.append((_dot(p.astype(_BF16), v_h) * inv_l).astype(_BF16))
    attn = jnp.concatenate(heads, axis=1)
    o_ref[...] = x + _dot(attn, wo_ref[...])


def _memattn(x2d, gain, w_q, kv, w_o, layer, seq, pre=None):
    n = x2d.shape[0]
    tm = TOKEN_TILE
    tiles_per_seq = seq // tm
    row_spec = pl.BlockSpec((tm, D_MODEL), lambda i: (i, 0))
    w_spec = pl.BlockSpec((D_MODEL, D_MODEL), lambda i: (0, 0))
    in_specs = [row_spec]
    args = [x2d]
    n_w = 2
    if pre is not None:
        in_specs += [row_spec, w_spec]
        args += list(pre)
        n_w = 3
    in_specs += [
        pl.BlockSpec((1, D_MODEL), lambda i: (0, 0)),
        w_spec,
        pl.BlockSpec((1, 1, MEM_LEN, 2 * D_MODEL), lambda i: (layer, i // tiles_per_seq, 0, 0)),
        w_spec,
    ]
    args += [gain.reshape(1, D_MODEL), w_q, kv, w_o]
    blk = 3 * tm * D_MODEL * 4 + n_w * D_MODEL * D_MODEL * 2 + MEM_LEN * 2 * D_MODEL * 2
    return pl.pallas_call(
        functools.partial(_memattn_kernel, has_pre=pre is not None),
        grid=(n // tm,),
        in_specs=in_specs,
        out_specs=row_spec,
        out_shape=jax.ShapeDtypeStruct((n, D_MODEL), _F32),
        compiler_params=_params(blk, 8 * tm * D_MODEL * 4, 1),
        name="memattn_l%d" % layer,
    )(*args)


def _ff_chunks():
    return [(c0, min(FF_CHUNK, D_FF - c0)) for c0 in range(0, D_FF, FF_CHUNK)]


def _swiglu_kernel(*refs, final):
    if final:
        x_ref, g_ref, wgu_ref, wd_ref, gf_ref, o_ref = refs
    else:
        x_ref, g_ref, wgu_ref, wd_ref, o_ref = refs
    x = x_ref[...]
    h = (_rms_unit(x) * g_ref[...]).astype(_BF16)
    acc = x
    for c0, cw in _ff_chunks():
        gate = _dot(h, wgu_ref[:, c0:c0 + cw])
        up = _dot(h, wgu_ref[:, D_FF + c0:D_FF + c0 + cw])
        act = (gate * (1.0 / (1.0 + jnp.exp(-gate))) * up).astype(_BF16)
        acc = acc + _dot(act, wd_ref[c0:c0 + cw, :])
    if final:
        acc = _rms_unit(acc) * gf_ref[...]
    o_ref[...] = acc


def _swiglu(x2d, gain, w_gu, w_down, final_gain=None):
    n = x2d.shape[0]
    tm = TOKEN_TILE
    g_spec = pl.BlockSpec((1, D_MODEL), lambda i: (0, 0))
    in_specs = [
        pl.BlockSpec((tm, D_MODEL), lambda i: (i, 0)),
        g_spec,
        pl.BlockSpec((D_MODEL, 2 * D_FF), lambda i: (0, 0), pipeline_mode=pl.Buffered(1)),
        pl.BlockSpec((D_FF, D_MODEL), lambda i: (0, 0), pipeline_mode=pl.Buffered(1)),
    ]
    args = [x2d, gain.reshape(1, D_MODEL), w_gu, w_down]
    if final_gain is not None:
        in_specs.append(g_spec)
        args.append(final_gain.reshape(1, D_MODEL))
    blk = 2 * tm * D_MODEL * 4 + 3 * D_MODEL * D_FF * 2 // 2
    return pl.pallas_call(
        functools.partial(_swiglu_kernel, final=final_gain is not None),
        grid=(n // tm,),
        in_specs=in_specs,
        out_specs=pl.BlockSpec((tm, D_MODEL), lambda i: (i, 0)),
        out_shape=jax.ShapeDtypeStruct((n, D_MODEL), _F32),
        compiler_params=_params(blk, 4 * tm * D_MODEL * 4 + 4 * tm * FF_CHUNK * 4, 1),
        name="swiglu_final" if final_gain is not None else "swiglu",
    )(*args)


def _rope(x, cos, sin_signed, first_half):
    d = x.shape[1]
    rot = jnp.where(first_half, pltpu.roll(x, d - ROT_DIM // 2, 1), pltpu.roll(x, ROT_DIM // 2, 1))
    return x * cos + rot * sin_signed


def _qkv_kernel(x_ref, pos_ref, invf_ref, gq_ref, gk_ref, wq_ref, wkv_ref,
                q_ref, k_ref, vt_ref, km_ref, *, tiles_per_seq):
    d = D_MODEL
    tm = x_ref.shape[0]
    blocks_per_tile = tm // BLOCK
    s_tile = pl.program_id(0) % tiles_per_seq

    @pl.when(s_tile == 0)
    def _():
        km_ref[...] = jnp.zeros_like(km_ref)

    xh = _rms_unit(x_ref[...])
    hq = (xh * gq_ref[...]).astype(_BF16)
    hk = (xh * gk_ref[...]).astype(_BF16)

    pos = pos_ref[0].astype(_F32)
    pos_rows = jnp.broadcast_to(pos, (V7X_LANES, tm)).T
    ang = pos_rows * invf_ref[...]
    lane = lax.broadcasted_iota(jnp.int32, (1, HEAD_DIM), 1)
    cos = jnp.cos(ang)
    sin_signed = jnp.where(lane < ROT_DIM // 2, -1.0, 1.0) * jnp.sin(ang)
    cos = jnp.concatenate([cos] * N_HEADS, axis=1)
    sin_signed = jnp.concatenate([sin_signed] * N_HEADS, axis=1)
    lane_d = lax.broadcasted_iota(jnp.int32, (1, d), 1)
    first_half = (lane_d % HEAD_DIM) < ROT_DIM // 2

    q = _dot(hq, wq_ref[...]) * (HEAD_DIM ** -0.5 * math.log2(math.e))
    q_ref[...] = _rope(q, cos, sin_signed, first_half).astype(_BF16)

    k = _rope(_dot(hk, wkv_ref[:, 0:d]), cos, sin_signed, first_half)
    k_ref[...] = k.astype(_BF16)
    v = _dot(hk, wkv_ref[:, d:2 * d])

    row16 = lax.broadcasted_iota(jnp.int32, km_ref.shape[1:], 0)
    for j in range(blocks_per_tile):
        r0 = j * BLOCK
        vt_ref[0, j] = v[r0:r0 + BLOCK, :].T.astype(_BF16)
        mean_row = jnp.sum(k[r0:r0 + BLOCK, :], axis=0, keepdims=True) * (1.0 / BLOCK)
        km_ref[0] = jnp.where(row16 == s_tile * blocks_per_tile + j, mean_row, km_ref[0])


def _qkv(x2d, positions, gain_q, gain_kv, w_q, w_kv, bsz, seq):
    n = x2d.shape[0]
    tm = TOKEN_TILE
    tiles_per_seq = seq // tm
    n_blocks = seq // BLOCK
    inv_freq = ROPE_THETA ** (-jnp.arange(0, ROT_DIM, 2, dtype=_F32) / ROT_DIM)
    invf_lane = jnp.concatenate(
        [inv_freq, inv_freq, jnp.zeros((HEAD_DIM - ROT_DIM,), _F32)]).reshape(1, HEAD_DIM)
    g_spec = pl.BlockSpec((1, D_MODEL), lambda i: (0, 0))
    row_spec = pl.BlockSpec((tm, D_MODEL), lambda i: (i, 0))
    blk = tm * D_MODEL * 4 + 3 * D_MODEL * D_MODEL * 2 + 3 * tm * D_MODEL * 2 + n_blocks * D_MODEL * 4
    return pl.pallas_call(
        functools.partial(_qkv_kernel, tiles_per_seq=tiles_per_seq),
        grid=(n // tm,),
        in_specs=[
            row_spec,
            pl.BlockSpec((1, 1, tm), lambda i: (i, 0, 0)),
            pl.BlockSpec((1, HEAD_DIM), lambda i: (0, 0)),
            g_spec, g_spec,
            pl.BlockSpec((D_MODEL, D_MODEL), lambda i: (0, 0)),
            pl.BlockSpec((D_MODEL, 2 * D_MODEL), lambda i: (0, 0)),
        ],
        out_specs=[
            row_spec,
            row_spec,
            pl.BlockSpec((1, tm // BLOCK, D_MODEL, BLOCK),
                         lambda i: (i // tiles_per_seq, i % tiles_per_seq, 0, 0)),
            pl.BlockSpec((1, n_blocks, D_MODEL), lambda i: (i // tiles_per_seq, 0, 0)),
        ],
        out_shape=[
            jax.ShapeDtypeStruct((n, D_MODEL), _BF16),
            jax.ShapeDtypeStruct((n, D_MODEL), _BF16),
            jax.ShapeDtypeStruct((bsz, n_blocks, D_MODEL, BLOCK), _BF16),
            jax.ShapeDtypeStruct((bsz, n_blocks, D_MODEL), _F32),
        ],
        compiler_params=_params(blk, 12 * tm * D_MODEL * 4, 1),
        name="qkv",
    )(x2d, positions.reshape(n // tm, 1, tm), invf_lane,
      gain_q.reshape(1, D_MODEL), gain_kv.reshape(1, D_MODEL), w_q, w_kv)


def _moba_kernel(q_ref, k_ref, vt_ref, km_ref, o_ref, bias_ref, m_ref, l_ref, acc_ref):
    t = pl.program_id(1)
    n_blocks = km_ref.shape[1]
    own_rows = pl.ds(pl.multiple_of(t * BLOCK, BLOCK), BLOCK)

    head_cols = [slice(h * HEAD_DIM, (h + 1) * HEAD_DIM) for h in range(N_HEADS)]

    gates = []
    for c in head_cols:
        km = km_ref[0, :, c]
        km_hi = km.astype(_BF16)
        km_lo = (km - km_hi.astype(_F32)).astype(_BF16)
        gates.append(lax.dot_general(km_hi, q_ref[:, c], _NT, preferred_element_type=_F32)
                     + lax.dot_general(km_lo, q_ref[:, c], _NT, preferred_element_type=_F32))
    own_scores = [lax.dot_general(k_ref[own_rows, c], q_ref[:, c], _NT, preferred_element_type=_F32)
                  for c in head_cols]

    blk = lax.broadcasted_iota(jnp.int32, (n_blocks, BLOCK), 0)
    for h in range(N_HEADS):
        gate = gates[h]
        rank = jnp.zeros(gate.shape, _F32)
        for m in range(n_blocks):
            g_m = gate[m:m + 1, :]
            beats = jnp.where(g_m > gate, 1.0, jnp.where(g_m == gate, jnp.where(blk > m, 1.0, 0.0), 0.0))
            rank = rank + jnp.where(m < t, beats, 0.0)
        selected = jnp.where(blk < t, rank, float(TOPK)) < TOPK
        bias_ref[h] = jnp.where(selected, 0.0, -jnp.inf)

    kpos = lax.broadcasted_iota(jnp.int32, (BLOCK, BLOCK), 0)
    qpos = lax.broadcasted_iota(jnp.int32, (BLOCK, BLOCK), 1)
    own_probs = []
    for h in range(N_HEADS):
        s = jnp.where(kpos <= qpos, own_scores[h], -jnp.inf)
        m0 = jnp.max(s, axis=0, keepdims=True)
        p = jnp.exp2(s - m0)
        m_ref[h:h + 1, :] = m0
        l_ref[h:h + 1, :] = jnp.sum(p, axis=0, keepdims=True)
        own_probs.append(p.astype(_BF16))
    for h in range(N_HEADS):
        acc_ref[h] = _dot(vt_ref[0, t, head_cols[h], :], own_probs[h])

    ones_rows = jnp.ones((_BF16_SUBLANES, BLOCK), _BF16)

    def past_block_pair(j, carry):
        n0 = 2 * j
        rows = pl.ds(pl.multiple_of(n0 * BLOCK, 2 * BLOCK), 2 * BLOCK)
        scores = [lax.dot_general(k_ref[rows, c], q_ref[:, c], _NT, preferred_element_type=_F32)
                  for c in head_cols]
        probs, alphas = [], []
        for h in range(N_HEADS):
            s_a, s_b = scores[h][:BLOCK], scores[h][BLOCK:]
            bias_a = bias_ref[h, pl.ds(n0, 1), :]
            bias_b = bias_ref[h, pl.ds(n0 + 1, 1), :]
            m_run = m_ref[h:h + 1, :]
            m_new = jnp.maximum(m_run, jnp.maximum(jnp.max(s_a, axis=0, keepdims=True) + bias_a,
                                                   jnp.max(s_b, axis=0, keepdims=True) + bias_b))
            m_ref[h:h + 1, :] = m_new
            alphas.append(jnp.exp2(m_run - m_new))
            probs.append((jnp.exp2(s_a - (m_new - bias_a)).astype(_BF16),
                          jnp.exp2(s_b - (m_new - bias_b)).astype(_BF16)))
        for h in range(N_HEADS):
            vt_a = jnp.concatenate([vt_ref[0, n0, head_cols[h], :], ones_rows], axis=0)
            vt_b = jnp.concatenate([vt_ref[0, n0 + 1, head_cols[h], :], ones_rows], axis=0)
            pv = _dot(vt_a, probs[h][0]) + _dot(vt_b, probs[h][1])
            acc_ref[h] = alphas[h] * acc_ref[h] + pv[:HEAD_DIM]
            l_ref[h:h + 1, :] = alphas[h] * l_ref[h:h + 1, :] + pv[HEAD_DIM:HEAD_DIM + 1]
        return carry

    lax.fori_loop(0, lax.shift_right_logical(t + 1, 1), past_block_pair, 0)
    for h in range(N_HEADS):
        out_h = acc_ref[h] * (1.0 / l_ref[h:h + 1, :])
        o_ref[:, h * HEAD_DIM:(h + 1) * HEAD_DIM] = out_h.T.astype(_BF16)


def _moba(q, k, vt, k_mean, bsz, seq):
    n = q.shape[0]
    n_blocks = seq // BLOCK
    blk = 2 * BLOCK * D_MODEL * 2 + 2 * seq * D_MODEL * 2 + n_blocks * D_MODEL * 4
    scratch = N_HEADS * (n_blocks + HEAD_DIM + 2) * BLOCK * 4
    return pl.pallas_call(
        _moba_kernel,
        grid=(bsz, n_blocks),
        in_specs=[
            pl.BlockSpec((BLOCK, D_MODEL), lambda b, t: (b * n_blocks + t, 0)),
            pl.BlockSpec((seq, D_MODEL), lambda b, t: (b, 0)),
            pl.BlockSpec((1, n_blocks, D_MODEL, BLOCK), lambda b, t: (b, 0, 0, 0)),
            pl.BlockSpec((1, n_blocks, D_MODEL), lambda b, t: (b, 0, 0)),
        ],
        out_specs=pl.BlockSpec((BLOCK, D_MODEL), lambda b, t: (b * n_blocks + t, 0)),
        out_shape=jax.ShapeDtypeStruct((n, D_MODEL), _BF16),
        scratch_shapes=[
            pltpu.VMEM((N_HEADS, n_blocks, BLOCK), _F32),
            pltpu.VMEM((N_HEADS, BLOCK), _F32),
            pltpu.VMEM((N_HEADS, BLOCK), _F32),
            pltpu.VMEM((N_HEADS, HEAD_DIM, BLOCK), _F32),
        ],
        compiler_params=_params(blk, scratch + 24 * BLOCK * BLOCK * 4, 2),
        name="moba",
    )(q, k, vt, k_mean)


def kernel(x, mem, positions, norm_mix, norm_mem, norm_memkv, norm_ffn, norm_final,
           conv_w_in, conv_w, conv_w_out, kv_norm, w_kv, moba_w_q, moba_w_o,
           mem_w_q, mem_w_kv, mem_w_o, ffn_w_gu, ffn_w_down):
    bsz, seq, d = x.shape
    assert d == D_MODEL and seq % TOKEN_TILE == 0 and TOKEN_TILE % BLOCK == 0
    assert mem.shape == (bsz, MEM_LEN, D_MODEL)
    assert norm_mix.shape[0] == DEPTH == 2 and conv_w_in.shape[0] == 1 and moba_w_q.shape[0] == 1
    bf = lambda w: w.astype(_BF16)

    x2d = x.reshape(bsz * seq, d)
    kv_mem = _memkv(mem, norm_memkv, bf(mem_w_kv))

    x2d = _convmix(x2d, norm_mix[0], bf(conv_w_in[0]), conv_w[0], bf(conv_w_out[0]), seq)
    x2d = _memattn(x2d, norm_mem[0], bf(mem_w_q[0]), kv_mem, bf(mem_w_o[0]), 0, seq)
    x2d = _swiglu(x2d, norm_ffn[0], bf(ffn_w_gu[0]), bf(ffn_w_down[0]))

    q, k, vt, k_mean = _qkv(x2d, positions, norm_mix[1], kv_norm, bf(moba_w_q[0]), bf(w_kv), bsz, seq)
    attn = _moba(q, k, vt, k_mean, bsz, seq)

    x2d = _memattn(x2d, norm_mem[1], bf(mem_w_q[1]), kv_mem, bf(mem_w_o[1]), 1, seq,
                   pre=(attn, bf(moba_w_o[0])))
    x2d = _swiglu(x2d, norm_ffn[1], bf(ffn_w_gu[1]), bf(ffn_w_down[1]), final_gain=norm_final)
    return x2d.reshape(bsz, seq, d)
```

```python
import functools
import math

import jax
import jax.numpy as jnp
from jax import lax
from jax.experimental import pallas as pl
from jax.experimental.pallas import tpu as pltpu

D_MODEL = 1024
DEPTH = 2
CONV_W = 3
N_HEADS = 8
HEAD_DIM = D_MODEL // N_HEADS
BLOCK = 256
TOPK = 3
ROT_DIM = HEAD_DIM // 4
ROPE_THETA = 500000.0
MEM_LEN = 256
MEM_HEADS = 4
MEM_HEAD_DIM = D_MODEL // MEM_HEADS
D_FF = ((8 * D_MODEL + 3 * 256 - 1) // (3 * 256)) * 256
EPS = 1e-6

V7X_VMEM_BYTES = 64 * 1024 * 1024
V7X_LANES = 128
V7X_SUBLANES = 8
V7X_MXU_DIM = 256
_BF16_SUBLANES = 2 * V7X_SUBLANES

TOKEN_TILE = 512
FF_CHUNK = 4 * V7X_MXU_DIM
MOBA_BATCHES_PER_STEP = 2
MOBA_SCORE_LOOKAHEAD = 4

_F32 = jnp.float32
_BF16 = jnp.bfloat16
_NT = (((1,), (1,)), ((), ()))


def _vmem_limit(block_bytes, temp_bytes):
    need = 2 * block_bytes + temp_bytes
    return int(min(need + need // 4, V7X_VMEM_BYTES - 8 * 1024 * 1024))


def _params(block_bytes, temp_bytes, ndim):
    return pltpu.CompilerParams(
        dimension_semantics=("arbitrary",) * ndim,
        vmem_limit_bytes=_vmem_limit(block_bytes, temp_bytes))


def _dot(a, b):
    return jnp.dot(a, b, preferred_element_type=_F32)


def _rms_unit(x):
    ms = jnp.mean(x * x, axis=-1, keepdims=True)
    return x * lax.rsqrt(ms + EPS)


def _memkv_kernel(mem_ref, g_ref, w_ref, kv_ref):
    h = (_rms_unit(mem_ref[0]) * g_ref[0]).astype(_BF16)
    kv_ref[0, 0] = _dot(h, w_ref[0]).astype(_BF16)


def _memkv(mem, norm_memkv, w_kv_bf16):
    bsz = mem.shape[0]
    blk = MEM_LEN * D_MODEL * 4 + D_MODEL * 2 * D_MODEL * 2 + MEM_LEN * 2 * D_MODEL * 2
    return pl.pallas_call(
        _memkv_kernel,
        grid=(DEPTH, bsz),
        in_specs=[
            pl.BlockSpec((1, MEM_LEN, D_MODEL), lambda l, b: (b, 0, 0)),
            pl.BlockSpec((1, 1, D_MODEL), lambda l, b: (l, 0, 0)),
            pl.BlockSpec((1, D_MODEL, 2 * D_MODEL), lambda l, b: (l, 0, 0)),
        ],
        out_specs=pl.BlockSpec((1, 1, MEM_LEN, 2 * D_MODEL), lambda l, b: (l, b, 0, 0)),
        out_shape=jax.ShapeDtypeStruct((DEPTH, bsz, MEM_LEN, 2 * D_MODEL), _BF16),
        compiler_params=_params(blk, 4 * MEM_LEN * 2 * D_MODEL * 4, 2),
        name="memkv",
    )(mem, norm_memkv.reshape(DEPTH, 1, D_MODEL), w_kv_bf16)


def _shift_rows(z, prev, k):
    zk = pltpu.roll(z, k, 0)
    pk = pltpu.roll(prev, k, 0)
    r = lax.broadcasted_iota(jnp.int32, prev.shape, 0)
    top = jnp.where(r < k, pk, zk[:V7X_SUBLANES])
    return jnp.concatenate([top, zk[V7X_SUBLANES:]], axis=0)


def _convmix_kernel(x_ref, g_ref, win_ref, cw_ref, wout_ref, o_ref, carry_ref, *, tiles_per_seq):
    d = D_MODEL

    @pl.when(pl.program_id(0) % tiles_per_seq == 0)
    def _():
        carry_ref[...] = jnp.zeros_like(carry_ref)

    x = x_ref[...]
    h = (_rms_unit(x) * g_ref[...]).astype(_BF16)
    b_gate = _dot(h, win_ref[:, 0:d])
    z = _dot(h, win_ref[:, d:2 * d]) * _dot(h, win_ref[:, 2 * d:3 * d])
    prev = carry_ref[...]
    carry_ref[...] = z[z.shape[0] - V7X_SUBLANES:, :]
    conv = (cw_ref[2:3, :] * z
            + cw_ref[1:2, :] * _shift_rows(z, prev, 1)
            + cw_ref[0:1, :] * _shift_rows(z, prev, 2))
    y = (b_gate * conv).astype(_BF16)
    o_ref[...] = x + _dot(y, wout_ref[...])


def _convmix(x2d, gain, w_in, w_conv, w_out, seq):
    n = x2d.shape[0]
    tm = TOKEN_TILE
    blk = 2 * tm * D_MODEL * 4 + 4 * D_MODEL * D_MODEL * 2
    return pl.pallas_call(
        functools.partial(_convmix_kernel, tiles_per_seq=seq // tm),
        grid=(n // tm,),
        in_specs=[
            pl.BlockSpec((tm, D_MODEL), lambda i: (i, 0)),
            pl.BlockSpec((1, D_MODEL), lambda i: (0, 0)),
            pl.BlockSpec((D_MODEL, 3 * D_MODEL), lambda i: (0, 0)),
            pl.BlockSpec((CONV_W, D_MODEL), lambda i: (0, 0)),
            pl.BlockSpec((D_MODEL, D_MODEL), lambda i: (0, 0)),
        ],
        out_specs=pl.BlockSpec((tm, D_MODEL), lambda i: (i, 0)),
        out_shape=jax.ShapeDtypeStruct((n, D_MODEL), _F32),
        scratch_shapes=[pltpu.VMEM((V7X_SUBLANES, D_MODEL), _F32)],
        compiler_params=_params(blk, 10 * tm * D_MODEL * 4, 1),
        name="convmix",
    )(x2d, gain.reshape(1, D_MODEL), w_in, w_conv, w_out)


def _memattn_kernel(*refs, has_pre):
    if has_pre:
        x_ref, a_ref, wpre_ref, g_ref, wq_ref, kv_ref, wo_ref, o_ref = refs
        x = x_ref[...] + _dot(a_ref[...], wpre_ref[...])
    else:
        x_ref, g_ref, wq_ref, kv_ref, wo_ref, o_ref = refs
        x = x_ref[...]
    h = (_rms_unit(x) * g_ref[...]).astype(_BF16)
    q = (_dot(h, wq_ref[...]) * (MEM_HEAD_DIM ** -0.5)).astype(_BF16)
    heads = []
    for hd in range(MEM_HEADS):
        c0 = hd * MEM_HEAD_DIM
        k_h = kv_ref[0, 0, :, c0:c0 + MEM_HEAD_DIM]
        v_h = kv_ref[0, 0, :, D_MODEL + c0:D_MODEL + c0 + MEM_HEAD_DIM]
        s = lax.dot_general(q[:, c0:c0 + MEM_HEAD_DIM], k_h, _NT, preferred_element_type=_F32)
        p = jnp.exp(s - jnp.max(s, axis=-1, keepdims=True))
        inv_l = 1.0 / jnp.sum(p, axis=-1, keepdims=True)
        heads.append((_dot(p.astype(_BF16), v_h) * inv_l).astype(_BF16))
    attn = jnp.concatenate(heads, axis=1)
    o_ref[...] = x + _dot(attn, wo_ref[...])


def _memattn(x2d, gain, w_q, kv, w_o, layer, seq, pre=None):
    n = x2d.shape[0]
    tm = TOKEN_TILE
    tiles_per_seq = seq // tm
    row_spec = pl.BlockSpec((tm, D_MODEL), lambda i: (i, 0))
    w_spec = pl.BlockSpec((D_MODEL, D_MODEL), lambda i: (0, 0))
    in_specs = [row_spec]
    args = [x2d]
    n_w = 2
    if pre is not None:
        in_specs += [row_spec, w_spec]
        args += list(pre)
        n_w = 3
    in_specs += [
        pl.BlockSpec((1, D_MODEL), lambda i: (0, 0)),
        w_spec,
        pl.BlockSpec((1, 1, MEM_LEN, 2 * D_MODEL), lambda i: (layer, i // tiles_per_seq, 0, 0)),
        w_spec,
    ]
    args += [gain.reshape(1, D_MODEL), w_q, kv, w_o]
    blk = 3 * tm * D_MODEL * 4 + n_w * D_MODEL * D_MODEL * 2 + MEM_LEN * 2 * D_MODEL * 2
    return pl.pallas_call(
        functools.partial(_memattn_kernel, has_pre=pre is not None),
        grid=(n // tm,),
        in_specs=in_specs,
        out_specs=row_spec,
        out_shape=jax.ShapeDtypeStruct((n, D_MODEL), _F32),
        compiler_params=_params(blk, 8 * tm * D_MODEL * 4, 1),
        name="memattn_l%d" % layer,
    )(*args)


def _ff_chunks():
    return [(c0, min(FF_CHUNK, D_FF - c0)) for c0 in range(0, D_FF, FF_CHUNK)]


def _swiglu_kernel(*refs, final):
    if final:
        x_ref, g_ref, wgu_ref, wd_ref, gf_ref, o_ref = refs
    else:
        x_ref, g_ref, wgu_ref, wd_ref, o_ref = refs
    x = x_ref[...]
    h = (_rms_unit(x) * g_ref[...]).astype(_BF16)
    acc = x
    for c0, cw in _ff_chunks():
        gate = _dot(h, wgu_ref[:, c0:c0 + cw])
        up = _dot(h, wgu_ref[:, D_FF + c0:D_FF + c0 + cw])
        act = (gate * (1.0 / (1.0 + jnp.exp(-gate))) * up).astype(_BF16)
        acc = acc + _dot(act, wd_ref[c0:c0 + cw, :])
    if final:
        acc = _rms_unit(acc) * gf_ref[...]
    o_ref[...] = acc


def _swiglu(x2d, gain, w_gu, w_down, final_gain=None):
    n = x2d.shape[0]
    tm = TOKEN_TILE
    g_spec = pl.BlockSpec((1, D_MODEL), lambda i: (0, 0))
    in_specs = [
        pl.BlockSpec((tm, D_MODEL), lambda i: (i, 0)),
        g_spec,
        pl.BlockSpec((D_MODEL, 2 * D_FF), lambda i: (0, 0), pipeline_mode=pl.Buffered(1)),
        pl.BlockSpec((D_FF, D_MODEL), lambda i: (0, 0), pipeline_mode=pl.Buffered(1)),
    ]
    args = [x2d, gain.reshape(1, D_MODEL), w_gu, w_down]
    if final_gain is not None:
        in_specs.append(g_spec)
        args.append(final_gain.reshape(1, D_MODEL))
    blk = 2 * tm * D_MODEL * 4 + 3 * D_MODEL * D_FF * 2 // 2
    return pl.pallas_call(
        functools.partial(_swiglu_kernel, final=final_gain is not None),
        grid=(n // tm,),
        in_specs=in_specs,
        out_specs=pl.BlockSpec((tm, D_MODEL), lambda i: (i, 0)),
        out_shape=jax.ShapeDtypeStruct((n, D_MODEL), _F32),
        compiler_params=_params(blk, 4 * tm * D_MODEL * 4 + 4 * tm * FF_CHUNK * 4, 1),
        name="swiglu_final" if final_gain is not None else "swiglu",
    )(*args)


def _rope(x, cos, sin_signed, first_half):
    d = x.shape[1]
    rot = jnp.where(first_half, pltpu.roll(x, d - ROT_DIM // 2, 1), pltpu.roll(x, ROT_DIM // 2, 1))
    return x * cos + rot * sin_signed


def _qkv_kernel(x_ref, pos_ref, invf_ref, gq_ref, gk_ref, wq_ref, wkv_ref,
                q_ref, k_ref, vt_ref, km_ref, *, tiles_per_seq):
    d = D_MODEL
    tm = x_ref.shape[0]
    blocks_per_tile = tm // BLOCK
    s_tile = pl.program_id(0) % tiles_per_seq

    @pl.when(s_tile == 0)
    def _():
        km_ref[...] = jnp.zeros_like(km_ref)

    xh = _rms_unit(x_ref[...])
    hq = (xh * gq_ref[...]).astype(_BF16)
    hk = (xh * gk_ref[...]).astype(_BF16)

    pos = pos_ref[0].astype(_F32)
    pos_rows = jnp.broadcast_to(pos, (V7X_LANES, tm)).T
    ang = pos_rows * invf_ref[...]
    lane = lax.broadcasted_iota(jnp.int32, (1, HEAD_DIM), 1)
    cos = jnp.cos(ang)
    sin_signed = jnp.where(lane < ROT_DIM // 2, -1.0, 1.0) * jnp.sin(ang)
    cos = jnp.concatenate([cos] * N_HEADS, axis=1)
    sin_signed = jnp.concatenate([sin_signed] * N_HEADS, axis=1)
    lane_d = lax.broadcasted_iota(jnp.int32, (1, d), 1)
    first_half = (lane_d % HEAD_DIM) < ROT_DIM // 2

    q = _dot(hq, wq_ref[...]) * (HEAD_DIM ** -0.5 * math.log2(math.e))
    q_ref[...] = _rope(q, cos, sin_signed, first_half).astype(_BF16)

    k = _rope(_dot(hk, wkv_ref[:, 0:d]), cos, sin_signed, first_half)
    k_ref[...] = k.astype(_BF16)
    v = _dot(hk, wkv_ref[:, d:2 * d])

    row16 = lax.broadcasted_iota(jnp.int32, km_ref.shape[1:], 0)
    for j in range(blocks_per_tile):
        r0 = j * BLOCK
        vt_ref[0, j] = v[r0:r0 + BLOCK, :].T.astype(_BF16)
        mean_row = jnp.sum(k[r0:r0 + BLOCK, :], axis=0, keepdims=True) * (1.0 / BLOCK)
        km_ref[0] = jnp.where(row16 == s_tile * blocks_per_tile + j, mean_row, km_ref[0])


def _qkv(x2d, positions, gain_q, gain_kv, w_q, w_kv, bsz, seq):
    n = x2d.shape[0]
    tm = TOKEN_TILE
    tiles_per_seq = seq // tm
    n_blocks = seq // BLOCK
    inv_freq = ROPE_THETA ** (-jnp.arange(0, ROT_DIM, 2, dtype=_F32) / ROT_DIM)
    invf_lane = jnp.concatenate(
        [inv_freq, inv_freq, jnp.zeros((HEAD_DIM - ROT_DIM,), _F32)]).reshape(1, HEAD_DIM)
    g_spec = pl.BlockSpec((1, D_MODEL), lambda i: (0, 0))
    row_spec = pl.BlockSpec((tm, D_MODEL), lambda i: (i, 0))
    blk = tm * D_MODEL * 4 + 3 * D_MODEL * D_MODEL * 2 + 3 * tm * D_MODEL * 2 + n_blocks * D_MODEL * 4
    return pl.pallas_call(
        functools.partial(_qkv_kernel, tiles_per_seq=tiles_per_seq),
        grid=(n // tm,),
        in_specs=[
            row_spec,
            pl.BlockSpec((1, 1, tm), lambda i: (i, 0, 0)),
            pl.BlockSpec((1, HEAD_DIM), lambda i: (0, 0)),
            g_spec, g_spec,
            pl.BlockSpec((D_MODEL, D_MODEL), lambda i: (0, 0)),
            pl.BlockSpec((D_MODEL, 2 * D_MODEL), lambda i: (0, 0)),
        ],
        out_specs=[
            row_spec,
            row_spec,
            pl.BlockSpec((1, tm // BLOCK, D_MODEL, BLOCK),
                         lambda i: (i // tiles_per_seq, i % tiles_per_seq, 0, 0)),
            pl.BlockSpec((1, n_blocks, D_MODEL), lambda i: (i // tiles_per_seq, 0, 0)),
        ],
        out_shape=[
            jax.ShapeDtypeStruct((n, D_MODEL), _BF16),
            jax.ShapeDtypeStruct((n, D_MODEL), _BF16),
            jax.ShapeDtypeStruct((bsz, n_blocks, D_MODEL, BLOCK), _BF16),
            jax.ShapeDtypeStruct((bsz, n_blocks, D_MODEL), _F32),
        ],
        compiler_params=_params(blk, 12 * tm * D_MODEL * 4, 1),
        name="qkv",
    )(x2d, positions.reshape(n // tm, 1, tm), invf_lane,
      gain_q.reshape(1, D_MODEL), gain_kv.reshape(1, D_MODEL), w_q, w_kv)


def _moba_kernel(q_ref, k_ref, vt_ref, km_ref, o_ref, bias_ref, m_ref, l_ref, acc_ref):
    t = pl.program_id(1)
    n_blocks = km_ref.shape[2]
    own_rows = pl.ds(pl.multiple_of(t * BLOCK, BLOCK), BLOCK)
    chains = [(a, slice(h * HEAD_DIM, (h + 1) * HEAD_DIM))
              for a in range(q_ref.shape[0]) for h in range(N_HEADS)]

    def score_t(i, rows):
        a, c = chains[i]
        return lax.dot_general(k_ref[a, 0, rows, c], q_ref[a, 0, :, c], _NT, preferred_element_type=_F32)

    gates = []
    for a, c in chains:
        km = km_ref[a, 0, :, c]
        km_hi = km.astype(_BF16)
        km_lo = (km - km_hi.astype(_F32)).astype(_BF16)
        gates.append(lax.dot_general(km_hi, q_ref[a, 0, :, c], _NT, preferred_element_type=_F32)
                     + lax.dot_general(km_lo, q_ref[a, 0, :, c], _NT, preferred_element_type=_F32))
    own_scores = [score_t(i, own_rows) for i in range(len(chains))]

    blk = lax.broadcasted_iota(jnp.int32, (n_blocks, BLOCK), 0)
    for i, gate in enumerate(gates):
        rank = jnp.zeros(gate.shape, _F32)
        for m in range(n_blocks):
            g_m = gate[m:m + 1, :]
            beats = jnp.where(g_m > gate, 1.0, jnp.where(g_m == gate, jnp.where(blk > m, 1.0, 0.0), 0.0))
            rank = rank + jnp.where(m < t, beats, 0.0)
        selected = jnp.where(blk < t, rank, float(TOPK)) < TOPK
        bias_ref[i] = jnp.where(selected, 0.0, -jnp.inf)

    kpos = lax.broadcasted_iota(jnp.int32, (BLOCK, BLOCK), 0)
    qpos = lax.broadcasted_iota(jnp.int32, (BLOCK, BLOCK), 1)
    own_probs = []
    for i in range(len(chains)):
        s = jnp.where(kpos <= qpos, own_scores[i], -jnp.inf)
        m0 = jnp.max(s, axis=0, keepdims=True)
        p = jnp.exp2(s - m0)
        m_ref[i:i + 1, :] = m0
        l_ref[i:i + 1, :] = jnp.sum(p, axis=0, keepdims=True)
        own_probs.append(p.astype(_BF16))
    for i, (a, c) in enumerate(chains):
        acc_ref[i] = _dot(vt_ref[a, 0, t, c, :], own_probs[i])

    ones_rows = jnp.ones((_BF16_SUBLANES, BLOCK), _BF16)

    def past_block_pair(j, carry):
        n0 = 2 * j
        rows = pl.ds(pl.multiple_of(n0 * BLOCK, 2 * BLOCK), 2 * BLOCK)
        scores = {i: score_t(i, rows) for i in range(MOBA_SCORE_LOOKAHEAD)}
        for i, (a, c) in enumerate(chains):
            s = scores.pop(i)
            s_a, s_b = s[:BLOCK], s[BLOCK:]
            bias_a = bias_ref[i, pl.ds(n0, 1), :]
            bias_b = bias_ref[i, pl.ds(n0 + 1, 1), :]
            m_run = m_ref[i:i + 1, :]
            m_new = jnp.maximum(m_run, jnp.maximum(jnp.max(s_a, axis=0, keepdims=True) + bias_a,
                                                   jnp.max(s_b, axis=0, keepdims=True) + bias_b))
            m_ref[i:i + 1, :] = m_new
            alpha = jnp.exp2(m_run - m_new)
            p_a = jnp.exp2(s_a - (m_new - bias_a)).astype(_BF16)
            p_b = jnp.exp2(s_b - (m_new - bias_b)).astype(_BF16)
            vt_a = jnp.concatenate([vt_ref[a, 0, n0, c, :], ones_rows], axis=0)
            vt_b = jnp.concatenate([vt_ref[a, 0, n0 + 1, c, :], ones_rows], axis=0)
            pv = _dot(vt_a, p_a) + _dot(vt_b, p_b)
            if i + MOBA_SCORE_LOOKAHEAD < len(chains):
                scores[i + MOBA_SCORE_LOOKAHEAD] = score_t(i + MOBA_SCORE_LOOKAHEAD, rows)
            acc_ref[i] = alpha * acc_ref[i] + pv[:HEAD_DIM]
            l_ref[i:i + 1, :] = alpha * l_ref[i:i + 1, :] + pv[HEAD_DIM:HEAD_DIM + 1]
        return carry

    lax.fori_loop(0, lax.shift_right_logical(t + 1, 1), past_block_pair, 0)
    for i, (a, c) in enumerate(chains):
        out = acc_ref[i] * (1.0 / l_ref[i:i + 1, :])
        o_ref[a, 0, :, c] = out.T.astype(_BF16)


def _moba(q, k, vt, k_mean, bsz, seq):
    n_blocks = seq // BLOCK
    bps = MOBA_BATCHES_PER_STEP
    groups = bsz // bps
    n_chains = bps * N_HEADS
    q4 = q.reshape(bps, groups, seq, D_MODEL)
    k4 = k.reshape(bps, groups, seq, D_MODEL)
    vt5 = vt.reshape(bps, groups, n_blocks, D_MODEL, BLOCK)
    km4 = k_mean.reshape(bps, groups, n_blocks, D_MODEL)
    single = pl.Buffered(1)
    blk = bps * (2 * BLOCK * D_MODEL * 2 + seq * D_MODEL * 2 + n_blocks * D_MODEL * 4)
    scratch = n_chains * (n_blocks + HEAD_DIM + 2) * BLOCK * 4
    out = pl.pallas_call(
        _moba_kernel,
        grid=(groups, n_blocks),
        in_specs=[
            pl.BlockSpec((bps, 1, BLOCK, D_MODEL), lambda g, t: (0, g, t, 0)),
            pl.BlockSpec((bps, 1, seq, D_MODEL), lambda g, t: (0, g, 0, 0), pipeline_mode=single),
            pl.BlockSpec((bps, 1, n_blocks, D_MODEL, BLOCK), lambda g, t: (0, g, 0, 0, 0),
                         pipeline_mode=single),
            pl.BlockSpec((bps, 1, n_blocks, D_MODEL), lambda g, t: (0, g, 0, 0)),
        ],
        out_specs=pl.BlockSpec((bps, 1, BLOCK, D_MODEL), lambda g, t: (0, g, t, 0)),
        out_shape=jax.ShapeDtypeStruct((bps, groups, seq, D_MODEL), _BF16),
        scratch_shapes=[
            pltpu.VMEM((n_chains, n_blocks, BLOCK), _F32),
            pltpu.VMEM((n_chains, BLOCK), _F32),
            pltpu.VMEM((n_chains, BLOCK), _F32),
            pltpu.VMEM((n_chains, HEAD_DIM, BLOCK), _F32),
        ],
        compiler_params=_params(blk, scratch + n_chains * 6 * BLOCK * BLOCK * 4, 2),
        name="moba",
    )(q4, k4, vt5, km4)
    return out.reshape(bsz * seq, D_MODEL)


def kernel(x, mem, positions, norm_mix, norm_mem, norm_memkv, norm_ffn, norm_final,
           conv_w_in, conv_w, conv_w_out, kv_norm, w_kv, moba_w_q, moba_w_o,
           mem_w_q, mem_w_kv, mem_w_o, ffn_w_gu, ffn_w_down):
    bsz, seq, d = x.shape
    assert d == D_MODEL and seq % TOKEN_TILE == 0 and TOKEN_TILE % BLOCK == 0
    assert mem.shape == (bsz, MEM_LEN, D_MODEL)
    assert bsz % MOBA_BATCHES_PER_STEP == 0
    assert norm_mix.shape[0] == DEPTH == 2 and conv_w_in.shape[0] == 1 and moba_w_q.shape[0] == 1
    bf = lambda w: w.astype(_BF16)

    x2d = x.reshape(bsz * seq, d)
    kv_mem = _memkv(mem, norm_memkv, bf(mem_w_kv))

    x2d = _convmix(x2d, norm_mix[0], bf(conv_w_in[0]), conv_w[0], bf(conv_w_out[0]), seq)
    x2d = _memattn(x2d, norm_mem[0], bf(mem_w_q[0]), kv_mem, bf(mem_w_o[0]), 0, seq)
    x2d = _swiglu(x2d, norm_ffn[0], bf(ffn_w_gu[0]), bf(ffn_w_down[0]))

    q, k, vt, k_mean = _qkv(x2d, positions, norm_mix[1], kv_norm, bf(moba_w_q[0]), bf(w_kv), bsz, seq)
    attn = _moba(q, k, vt, k_mean, bsz, seq)

    x2d = _memattn(x2d, norm_mem[1], bf(mem_w_q[1]), kv_mem, bf(mem_w_o[1]), 1, seq,
                   pre=(attn, bf(moba_w_o[0])))
    x2d = _swiglu(x2d, norm_ffn[1], bf(ffn_w_gu[1]), bf(ffn_w_down[1]), final_gain=norm_final)
    return x2d.reshape(bsz, seq, d)
```

```python
import functools
import math

import jax
import jax.numpy as jnp
from jax import lax
from jax.experimental import pallas as pl
from jax.experimental.pallas import tpu as pltpu

D_MODEL = 1024
DEPTH = 2
CONV_W = 3
N_HEADS = 8
HEAD_DIM = D_MODEL // N_HEADS
BLOCK = 256
TOPK = 3
ROT_DIM = HEAD_DIM // 4
ROPE_THETA = 500000.0
MEM_LEN = 256
MEM_HEADS = 4
MEM_HEAD_DIM = D_MODEL // MEM_HEADS
D_FF = ((8 * D_MODEL + 3 * 256 - 1) // (3 * 256)) * 256
EPS = 1e-6

V7X_VMEM_BYTES = 64 * 1024 * 1024
V7X_LANES = 128
V7X_SUBLANES = 8
V7X_MXU_DIM = 256
_BF16_SUBLANES = 2 * V7X_SUBLANES

TOKEN_TILE = 1024
QKV_TOKEN_TILE = 512
SUB_TILE = 512
FF_CHUNK = 4 * V7X_MXU_DIM
MOBA_BATCHES_PER_STEP = 2
MOBA_SCORE_LOOKAHEAD = 4

_F32 = jnp.float32
_BF16 = jnp.bfloat16
_NT = (((1,), (1,)), ((), ()))


def _vmem_limit(block_bytes, temp_bytes):
    need = 2 * block_bytes + temp_bytes
    return int(min(need + need // 4, V7X_VMEM_BYTES - 8 * 1024 * 1024))


def _params(block_bytes, temp_bytes, ndim):
    return pltpu.CompilerParams(
        dimension_semantics=("arbitrary",) * ndim,
        vmem_limit_bytes=_vmem_limit(block_bytes, temp_bytes))


def _dot(a, b):
    return jnp.dot(a, b, preferred_element_type=_F32)


def _sub_tiles(rows):
    return [slice(r0, r0 + SUB_TILE) for r0 in range(0, rows, SUB_TILE)]


def _rms_unit(x):
    ms = jnp.mean(x * x, axis=-1, keepdims=True)
    return x * lax.rsqrt(ms + EPS)


def _memkv_kernel(mem_ref, g_ref, w_ref, kv_ref):
    h = (_rms_unit(mem_ref[0]) * g_ref[0]).astype(_BF16)
    kv_ref[0, 0] = _dot(h, w_ref[0]).astype(_BF16)


def _memkv(mem, norm_memkv, w_kv_bf16):
    bsz = mem.shape[0]
    blk = MEM_LEN * D_MODEL * 4 + D_MODEL * 2 * D_MODEL * 2 + MEM_LEN * 2 * D_MODEL * 2
    return pl.pallas_call(
        _memkv_kernel,
        grid=(DEPTH, bsz),
        in_specs=[
            pl.BlockSpec((1, MEM_LEN, D_MODEL), lambda l, b: (b, 0, 0)),
            pl.BlockSpec((1, 1, D_MODEL), lambda l, b: (l, 0, 0)),
            pl.BlockSpec((1, D_MODEL, 2 * D_MODEL), lambda l, b: (l, 0, 0)),
        ],
        out_specs=pl.BlockSpec((1, 1, MEM_LEN, 2 * D_MODEL), lambda l, b: (l, b, 0, 0)),
        out_shape=jax.ShapeDtypeStruct((DEPTH, bsz, MEM_LEN, 2 * D_MODEL), _BF16),
        compiler_params=_params(blk, 4 * MEM_LEN * 2 * D_MODEL * 4, 2),
        name="memkv",
    )(mem, norm_memkv.reshape(DEPTH, 1, D_MODEL), w_kv_bf16)


def _shift_rows(z, prev, k):
    zk = pltpu.roll(z, k, 0)
    pk = pltpu.roll(prev, k, 0)
    r = lax.broadcasted_iota(jnp.int32, prev.shape, 0)
    top = jnp.where(r < k, pk, zk[:V7X_SUBLANES])
    return jnp.concatenate([top, zk[V7X_SUBLANES:]], axis=0)


def _convmix_kernel(x_ref, g_ref, win_ref, cw_ref, wout_ref, o_ref, carry_ref, *, tiles_per_seq):
    d = D_MODEL

    @pl.when(pl.program_id(0) % tiles_per_seq == 0)
    def _():
        carry_ref[...] = jnp.zeros_like(carry_ref)

    for rows in _sub_tiles(x_ref.shape[0]):
        x = x_ref[rows, :]
        h = (_rms_unit(x) * g_ref[...]).astype(_BF16)
        b_gate = _dot(h, win_ref[:, 0:d])
        z = _dot(h, win_ref[:, d:2 * d]) * _dot(h, win_ref[:, 2 * d:3 * d])
        prev = carry_ref[...]
        carry_ref[...] = z[z.shape[0] - V7X_SUBLANES:, :]
        conv = (cw_ref[2:3, :] * z
                + cw_ref[1:2, :] * _shift_rows(z, prev, 1)
                + cw_ref[0:1, :] * _shift_rows(z, prev, 2))
        y = (b_gate * conv).astype(_BF16)
        o_ref[rows, :] = x + _dot(y, wout_ref[...])


def _convmix(x2d, gain, w_in, w_conv, w_out, seq):
    n = x2d.shape[0]
    tm = TOKEN_TILE
    blk = 2 * tm * D_MODEL * 4 + 4 * D_MODEL * D_MODEL * 2
    return pl.pallas_call(
        functools.partial(_convmix_kernel, tiles_per_seq=seq // tm),
        grid=(n // tm,),
        in_specs=[
            pl.BlockSpec((tm, D_MODEL), lambda i: (i, 0)),
            pl.BlockSpec((1, D_MODEL), lambda i: (0, 0)),
            pl.BlockSpec((D_MODEL, 3 * D_MODEL), lambda i: (0, 0)),
            pl.BlockSpec((CONV_W, D_MODEL), lambda i: (0, 0)),
            pl.BlockSpec((D_MODEL, D_MODEL), lambda i: (0, 0)),
        ],
        out_specs=pl.BlockSpec((tm, D_MODEL), lambda i: (i, 0)),
        out_shape=jax.ShapeDtypeStruct((n, D_MODEL), _F32),
        scratch_shapes=[pltpu.VMEM((V7X_SUBLANES, D_MODEL), _F32)],
        compiler_params=_params(blk, 10 * SUB_TILE * D_MODEL * 4, 1),
        name="convmix",
    )(x2d, gain.reshape(1, D_MODEL), w_in, w_conv, w_out)


def _memattn_kernel(*refs, has_pre):
    if has_pre:
        x_ref, a_ref, wpre_ref, g_ref, wq_ref, kv_ref, wo_ref, o_ref = refs
    else:
        x_ref, g_ref, wq_ref, kv_ref, wo_ref, o_ref = refs
    for rows in _sub_tiles(x_ref.shape[0]):
        x = x_ref[rows, :]
        if has_pre:
            x = x + _dot(a_ref[rows, :], wpre_ref[...])
        h = (_rms_unit(x) * g_ref[...]).astype(_BF16)
        q = (_dot(h, wq_ref[...]) * (MEM_HEAD_DIM ** -0.5)).astype(_BF16)
        heads = []
        for hd in range(MEM_HEADS):
            c0 = hd * MEM_HEAD_DIM
            k_h = kv_ref[0, 0, :, c0:c0 + MEM_HEAD_DIM]
            v_h = kv_ref[0, 0, :, D_MODEL + c0:D_MODEL + c0 + MEM_HEAD_DIM]
            s = lax.dot_general(q[:, c0:c0 + MEM_HEAD_DIM], k_h, _NT, preferred_element_type=_F32)
            p = jnp.exp(s - jnp.max(s, axis=-1, keepdims=True))
            inv_l = 1.0 / jnp.sum(p, axis=-1, keepdims=True)
            heads.append((_dot(p.astype(_BF16), v_h) * inv_l).astype(_BF16))
        attn = jnp.concatenate(heads, axis=1)
        o_ref[rows, :] = x + _dot(attn, wo_ref[...])


def _memattn(x2d, gain, w_q, kv, w_o, layer, seq, pre=None):
    n = x2d.shape[0]
    tm = TOKEN_TILE
    tiles_per_seq = seq // tm
    row_spec = pl.BlockSpec((tm, D_MODEL), lambda i: (i, 0))
    w_spec = pl.BlockSpec((D_MODEL, D_MODEL), lambda i: (0, 0))
    in_specs = [row_spec]
    args = [x2d]
    n_w = 2
    if pre is not None:
        in_specs += [row_spec, w_spec]
        args += list(pre)
        n_w = 3
    in_specs += [
        pl.BlockSpec((1, D_MODEL), lambda i: (0, 0)),
        w_spec,
        pl.BlockSpec((1, 1, MEM_LEN, 2 * D_MODEL), lambda i: (layer, i // tiles_per_seq, 0, 0)),
        w_spec,
    ]
    args += [gain.reshape(1, D_MODEL), w_q, kv, w_o]
    blk = 3 * tm * D_MODEL * 4 + n_w * D_MODEL * D_MODEL * 2 + MEM_LEN * 2 * D_MODEL * 2
    return pl.pallas_call(
        functools.partial(_memattn_kernel, has_pre=pre is not None),
        grid=(n // tm,),
        in_specs=in_specs,
        out_specs=row_spec,
        out_shape=jax.ShapeDtypeStruct((n, D_MODEL), _F32),
        compiler_params=_params(blk, 8 * SUB_TILE * D_MODEL * 4, 1),
        name="memattn_l%d" % layer,
    )(*args)


def _ff_chunks():
    return [(c0, min(FF_CHUNK, D_FF - c0)) for c0 in range(0, D_FF, FF_CHUNK)]


def _swiglu_kernel(*refs, final):
    if final:
        x_ref, g_ref, wgu_ref, wd_ref, gf_ref, o_ref = refs
    else:
        x_ref, g_ref, wgu_ref, wd_ref, o_ref = refs
    for rows in _sub_tiles(x_ref.shape[0]):
        x = x_ref[rows, :]
        h = (_rms_unit(x) * g_ref[...]).astype(_BF16)
        acc = x
        for c0, cw in _ff_chunks():
            gate = _dot(h, wgu_ref[:, c0:c0 + cw])
            up = _dot(h, wgu_ref[:, D_FF + c0:D_FF + c0 + cw])
            act = (gate * (1.0 / (1.0 + jnp.exp(-gate))) * up).astype(_BF16)
            acc = acc + _dot(act, wd_ref[c0:c0 + cw, :])
        if final:
            acc = _rms_unit(acc) * gf_ref[...]
        o_ref[rows, :] = acc


def _swiglu(x2d, gain, w_gu, w_down, final_gain=None):
    n = x2d.shape[0]
    tm = TOKEN_TILE
    g_spec = pl.BlockSpec((1, D_MODEL), lambda i: (0, 0))
    in_specs = [
        pl.BlockSpec((tm, D_MODEL), lambda i: (i, 0)),
        g_spec,
        pl.BlockSpec((D_MODEL, 2 * D_FF), lambda i: (0, 0), pipeline_mode=pl.Buffered(1)),
        pl.BlockSpec((D_FF, D_MODEL), lambda i: (0, 0), pipeline_mode=pl.Buffered(1)),
    ]
    args = [x2d, gain.reshape(1, D_MODEL), w_gu, w_down]
    if final_gain is not None:
        in_specs.append(g_spec)
        args.append(final_gain.reshape(1, D_MODEL))
    blk = 2 * tm * D_MODEL * 4 + 3 * D_MODEL * D_FF * 2 // 2
    return pl.pallas_call(
        functools.partial(_swiglu_kernel, final=final_gain is not None),
        grid=(n // tm,),
        in_specs=in_specs,
        out_specs=pl.BlockSpec((tm, D_MODEL), lambda i: (i, 0)),
        out_shape=jax.ShapeDtypeStruct((n, D_MODEL), _F32),
        compiler_params=_params(blk, 4 * SUB_TILE * D_MODEL * 4 + 4 * SUB_TILE * FF_CHUNK * 4, 1),
        name="swiglu_final" if final_gain is not None else "swiglu",
    )(*args)


def _rope(x, cos, sin_signed, first_half):
    d = x.shape[1]
    rot = jnp.where(first_half, pltpu.roll(x, d - ROT_DIM // 2, 1), pltpu.roll(x, ROT_DIM // 2, 1))
    return x * cos + rot * sin_signed


def _qkv_kernel(x_ref, pos_ref, invf_ref, gq_ref, gk_ref, wq_ref, wkv_ref,
                q_ref, k_ref, vt_ref, km_ref, *, tiles_per_seq):
    d = D_MODEL
    tm = x_ref.shape[0]
    blocks_per_tile = tm // BLOCK
    s_tile = pl.program_id(0) % tiles_per_seq

    @pl.when(s_tile == 0)
    def _():
        km_ref[...] = jnp.zeros_like(km_ref)

    lane = lax.broadcasted_iota(jnp.int32, (1, HEAD_DIM), 1)
    lane_d = lax.broadcasted_iota(jnp.int32, (1, d), 1)
    first_half = (lane_d % HEAD_DIM) < ROT_DIM // 2
    row16 = lax.broadcasted_iota(jnp.int32, km_ref.shape[1:], 0)

    for rows in _sub_tiles(tm):
        xh = _rms_unit(x_ref[rows, :])
        hq = (xh * gq_ref[...]).astype(_BF16)
        hk = (xh * gk_ref[...]).astype(_BF16)

        pos = pos_ref[0, :, rows].astype(_F32)
        pos_rows = jnp.broadcast_to(pos, (V7X_LANES, SUB_TILE)).T
        ang = pos_rows * invf_ref[...]
        cos = jnp.cos(ang)
        sin_signed = jnp.where(lane < ROT_DIM // 2, -1.0, 1.0) * jnp.sin(ang)
        cos = jnp.concatenate([cos] * N_HEADS, axis=1)
        sin_signed = jnp.concatenate([sin_signed] * N_HEADS, axis=1)

        q = _dot(hq, wq_ref[...]) * (HEAD_DIM ** -0.5 * math.log2(math.e))
        q_ref[rows, :] = _rope(q, cos, sin_signed, first_half).astype(_BF16)

        k = _rope(_dot(hk, wkv_ref[:, 0:d]), cos, sin_signed, first_half)
        k_ref[rows, :] = k.astype(_BF16)
        v = _dot(hk, wkv_ref[:, d:2 * d])

        for r0 in range(0, SUB_TILE, BLOCK):
            j = (rows.start + r0) // BLOCK
            vt_ref[0, j] = v[r0:r0 + BLOCK, :].T.astype(_BF16)
            mean_row = jnp.sum(k[r0:r0 + BLOCK, :], axis=0, keepdims=True) * (1.0 / BLOCK)
            km_ref[0] = jnp.where(row16 == s_tile * blocks_per_tile + j, mean_row, km_ref[0])


def _qkv(x2d, positions, gain_q, gain_kv, w_q, w_kv, bsz, seq):
    n = x2d.shape[0]
    tm = QKV_TOKEN_TILE
    tiles_per_seq = seq // tm
    n_blocks = seq // BLOCK
    inv_freq = ROPE_THETA ** (-jnp.arange(0, ROT_DIM, 2, dtype=_F32) / ROT_DIM)
    invf_lane = jnp.concatenate(
        [inv_freq, inv_freq, jnp.zeros((HEAD_DIM - ROT_DIM,), _F32)]).reshape(1, HEAD_DIM)
    g_spec = pl.BlockSpec((1, D_MODEL), lambda i: (0, 0))
    row_spec = pl.BlockSpec((tm, D_MODEL), lambda i: (i, 0))
    blk = tm * D_MODEL * 4 + 3 * D_MODEL * D_MODEL * 2 + 3 * tm * D_MODEL * 2 + n_blocks * D_MODEL * 4
    return pl.pallas_call(
        functools.partial(_qkv_kernel, tiles_per_seq=tiles_per_seq),
        grid=(n // tm,),
        in_specs=[
            row_spec,
            pl.BlockSpec((1, 1, tm), lambda i: (i, 0, 0)),
            pl.BlockSpec((1, HEAD_DIM), lambda i: (0, 0)),
            g_spec, g_spec,
            pl.BlockSpec((D_MODEL, D_MODEL), lambda i: (0, 0)),
            pl.BlockSpec((D_MODEL, 2 * D_MODEL), lambda i: (0, 0)),
        ],
        out_specs=[
            row_spec,
            row_spec,
            pl.BlockSpec((1, tm // BLOCK, D_MODEL, BLOCK),
                         lambda i: (i // tiles_per_seq, i % tiles_per_seq, 0, 0)),
            pl.BlockSpec((1, n_blocks, D_MODEL), lambda i: (i // tiles_per_seq, 0, 0)),
        ],
        out_shape=[
            jax.ShapeDtypeStruct((n, D_MODEL), _BF16),
            jax.ShapeDtypeStruct((n, D_MODEL), _BF16),
            jax.ShapeDtypeStruct((bsz, n_blocks, D_MODEL, BLOCK), _BF16),
            jax.ShapeDtypeStruct((bsz, n_blocks, D_MODEL), _F32),
        ],
        compiler_params=_params(blk, 12 * SUB_TILE * D_MODEL * 4, 1),
        name="qkv",
    )(x2d, positions.reshape(n // tm, 1, tm), invf_lane,
      gain_q.reshape(1, D_MODEL), gain_kv.reshape(1, D_MODEL), w_q, w_kv)


def _moba_kernel(q_ref, k_ref, vt_ref, km_ref, o_ref, bias_ref, m_ref, l_ref, acc_ref):
    t = pl.program_id(1)
    n_blocks = km_ref.shape[2]
    own_rows = pl.ds(pl.multiple_of(t * BLOCK, BLOCK), BLOCK)
    chains = [(a, slice(h * HEAD_DIM, (h + 1) * HEAD_DIM))
              for a in range(q_ref.shape[0]) for h in range(N_HEADS)]

    def score_t(i, rows):
        a, c = chains[i]
        return lax.dot_general(k_ref[a, 0, rows, c], q_ref[a, 0, :, c], _NT, preferred_element_type=_F32)

    gates = []
    for a, c in chains:
        km = km_ref[a, 0, :, c]
        km_hi = km.astype(_BF16)
        km_lo = (km - km_hi.astype(_F32)).astype(_BF16)
        gates.append(lax.dot_general(km_hi, q_ref[a, 0, :, c], _NT, preferred_element_type=_F32)
                     + lax.dot_general(km_lo, q_ref[a, 0, :, c], _NT, preferred_element_type=_F32))
    own_scores = [score_t(i, own_rows) for i in range(len(chains))]

    blk = lax.broadcasted_iota(jnp.int32, (n_blocks, BLOCK), 0)
    for i, gate in enumerate(gates):
        rank = jnp.zeros(gate.shape, _F32)
        for m in range(n_blocks):
            g_m = gate[m:m + 1, :]
            beats = jnp.where(g_m > gate, 1.0, jnp.where(g_m == gate, jnp.where(blk > m, 1.0, 0.0), 0.0))
            rank = rank + jnp.where(m < t, beats, 0.0)
        selected = jnp.where(blk < t, rank, float(TOPK)) < TOPK
        bias_ref[i] = jnp.where(selected, 0.0, -jnp.inf)

    kpos = lax.broadcasted_iota(jnp.int32, (BLOCK, BLOCK), 0)
    qpos = lax.broadcasted_iota(jnp.int32, (BLOCK, BLOCK), 1)
    own_probs = []
    for i in range(len(chains)):
        s = jnp.where(kpos <= qpos, own_scores[i], -jnp.inf)
        m0 = jnp.max(s, axis=0, keepdims=True)
        p = jnp.exp2(s - m0)
        m_ref[i:i + 1, :] = m0
        l_ref[i:i + 1, :] = jnp.sum(p, axis=0, keepdims=True)
        own_probs.append(p.astype(_BF16))
    for i, (a, c) in enumerate(chains):
        acc_ref[i] = _dot(vt_ref[a, 0, t, c, :], own_probs[i])

    ones_rows = jnp.ones((_BF16_SUBLANES, BLOCK), _BF16)

    def past_block_pair(j, carry):
        n0 = 2 * j
        rows = pl.ds(pl.multiple_of(n0 * BLOCK, 2 * BLOCK), 2 * BLOCK)
        scores = {i: score_t(i, rows) for i in range(MOBA_SCORE_LOOKAHEAD)}
        for i, (a, c) in enumerate(chains):
            s = scores.pop(i)
            s_a, s_b = s[:BLOCK], s[BLOCK:]
            bias_a = bias_ref[i, pl.ds(n0, 1), :]
            bias_b = bias_ref[i, pl.ds(n0 + 1, 1), :]
            m_run = m_ref[i:i + 1, :]
            m_new = jnp.maximum(m_run, jnp.maximum(jnp.max(s_a, axis=0, keepdims=True) + bias_a,
                                                   jnp.max(s_b, axis=0, keepdims=True) + bias_b))
            m_ref[i:i + 1, :] = m_new
            alpha = jnp.exp2(m_run - m_new)
            p_a = jnp.exp2(s_a - (m_new - bias_a)).astype(_BF16)
            p_b = jnp.exp2(s_b - (m_new - bias_b)).astype(_BF16)
            vt_a = jnp.concatenate([vt_ref[a, 0, n0, c, :], ones_rows], axis=0)
            vt_b = jnp.concatenate([vt_ref[a, 0, n0 + 1, c, :], ones_rows], axis=0)
            pv = _dot(vt_a, p_a) + _dot(vt_b, p_b)
            if i + MOBA_SCORE_LOOKAHEAD < len(chains):
                scores[i + MOBA_SCORE_LOOKAHEAD] = score_t(i + MOBA_SCORE_LOOKAHEAD, rows)
            acc_ref[i] = alpha * acc_ref[i] + pv[:HEAD_DIM]
            l_ref[i:i + 1, :] = alpha * l_ref[i:i + 1, :] + pv[HEAD_DIM:HEAD_DIM + 1]
        return carry

    lax.fori_loop(0, lax.shift_right_logical(t + 1, 1), past_block_pair, 0)
    for i, (a, c) in enumerate(chains):
        out = acc_ref[i] * (1.0 / l_ref[i:i + 1, :])
        o_ref[a, 0, :, c] = out.T.astype(_BF16)


def _moba(q, k, vt, k_mean, bsz, seq):
    n_blocks = seq // BLOCK
    bps = MOBA_BATCHES_PER_STEP
    groups = bsz // bps
    n_chains = bps * N_HEADS
    q4 = q.reshape(bps, groups, seq, D_MODEL)
    k4 = k.reshape(bps, groups, seq, D_MODEL)
    vt5 = vt.reshape(bps, groups, n_blocks, D_MODEL, BLOCK)
    km4 = k_mean.reshape(bps, groups, n_blocks, D_MODEL)
    single = pl.Buffered(1)
    blk = bps * (2 * BLOCK * D_MODEL * 2 + seq * D_MODEL * 2 + n_blocks * D_MODEL * 4)
    scratch = n_chains * (n_blocks + HEAD_DIM + 2) * BLOCK * 4
    out = pl.pallas_call(
        _moba_kernel,
        grid=(groups, n_blocks),
        in_specs=[
            pl.BlockSpec((bps, 1, BLOCK, D_MODEL), lambda g, t: (0, g, t, 0)),
            pl.BlockSpec((bps, 1, seq, D_MODEL), lambda g, t: (0, g, 0, 0), pipeline_mode=single),
            pl.BlockSpec((bps, 1, n_blocks, D_MODEL, BLOCK), lambda g, t: (0, g, 0, 0, 0),
                         pipeline_mode=single),
            pl.BlockSpec((bps, 1, n_blocks, D_MODEL), lambda g, t: (0, g, 0, 0)),
        ],
        out_specs=pl.BlockSpec((bps, 1, BLOCK, D_MODEL), lambda g, t: (0, g, t, 0)),
        out_shape=jax.ShapeDtypeStruct((bps, groups, seq, D_MODEL), _BF16),
        scratch_shapes=[
            pltpu.VMEM((n_chains, n_blocks, BLOCK), _F32),
            pltpu.VMEM((n_chains, BLOCK), _F32),
            pltpu.VMEM((n_chains, BLOCK), _F32),
            pltpu.VMEM((n_chains, HEAD_DIM, BLOCK), _F32),
        ],
        compiler_params=_params(blk, scratch + n_chains * 6 * BLOCK * BLOCK * 4, 2),
        name="moba",
    )(q4, k4, vt5, km4)
    return out.reshape(bsz * seq, D_MODEL)


def kernel(x, mem, positions, norm_mix, norm_mem, norm_memkv, norm_ffn, norm_final,
           conv_w_in, conv_w, conv_w_out, kv_norm, w_kv, moba_w_q, moba_w_o,
           mem_w_q, mem_w_kv, mem_w_o, ffn_w_gu, ffn_w_down):
    bsz, seq, d = x.shape
    assert d == D_MODEL and seq % TOKEN_TILE == 0 and seq % QKV_TOKEN_TILE == 0
    assert TOKEN_TILE % SUB_TILE == 0 and QKV_TOKEN_TILE % SUB_TILE == 0 and SUB_TILE % BLOCK == 0
    assert mem.shape == (bsz, MEM_LEN, D_MODEL)
    assert bsz % MOBA_BATCHES_PER_STEP == 0
    assert norm_mix.shape[0] == DEPTH == 2 and conv_w_in.shape[0] == 1 and moba_w_q.shape[0] == 1
    bf = lambda w: w.astype(_BF16)

    x2d = x.reshape(bsz * seq, d)
    kv_mem = _memkv(mem, norm_memkv, bf(mem_w_kv))

    x2d = _convmix(x2d, norm_mix[0], bf(conv_w_in[0]), conv_w[0], bf(conv_w_out[0]), seq)
    x2d = _memattn(x2d, norm_mem[0], bf(mem_w_q[0]), kv_mem, bf(mem_w_o[0]), 0, seq)
    x2d = _swiglu(x2d, norm_ffn[0], bf(ffn_w_gu[0]), bf(ffn_w_down[0]))

    q, k, vt, k_mean = _qkv(x2d, positions, norm_mix[1], kv_norm, bf(moba_w_q[0]), bf(w_kv), bsz, seq)
    attn = _moba(q, k, vt, k_mean, bsz, seq)

    x2d = _memattn(x2d, norm_mem[1], bf(mem_w_q[1]), kv_mem, bf(mem_w_o[1]), 1, seq,
                   pre=(attn, bf(moba_w_o[0])))
    x2d = _swiglu(x2d, norm_ffn[1], bf(ffn_w_gu[1]), bf(ffn_w_down[1]), final_gain=norm_final)
    return x2d.reshape(bsz, seq, d)
```

```python
import functools
import math

import jax
import jax.numpy as jnp
from jax import lax
from jax.experimental import pallas as pl
from jax.experimental.pallas import tpu as pltpu

D_MODEL = 1024
DEPTH = 2
CONV_W = 3
N_HEADS = 8
HEAD_DIM = D_MODEL // N_HEADS
BLOCK = 256
TOPK = 3
ROT_DIM = HEAD_DIM // 4
ROPE_THETA = 500000.0
MEM_LEN = 256
MEM_HEADS = 4
MEM_HEAD_DIM = D_MODEL // MEM_HEADS
D_FF = ((8 * D_MODEL + 3 * 256 - 1) // (3 * 256)) * 256
EPS = 1e-6

V7X_VMEM_BYTES = 64 * 1024 * 1024
V7X_LANES = 128
V7X_SUBLANES = 8
V7X_MXU_DIM = 256
_BF16_SUBLANES = 2 * V7X_SUBLANES

TOKEN_TILE = 1024
QKV_TOKEN_TILE = 512
SUB_TILE = 512
FF_CHUNK = 4 * V7X_MXU_DIM
MOBA_BATCHES_PER_STEP = 2
MOBA_SCORE_LOOKAHEAD = 4
STAGE_CHUNK_BYTES = 3 * 1024 * 1024

_F32 = jnp.float32
_BF16 = jnp.bfloat16
_NT = (((1,), (1,)), ((), ()))
_HBM = pl.BlockSpec(memory_space=pl.ANY)


def _vmem_limit(block_bytes, resident_bytes):
    need = 2 * block_bytes + resident_bytes
    return int(min(need + need // 4, V7X_VMEM_BYTES - 8 * 1024 * 1024))


def _params(block_bytes, resident_bytes, ndim):
    return pltpu.CompilerParams(
        dimension_semantics=("arbitrary",) * ndim,
        vmem_limit_bytes=_vmem_limit(block_bytes, resident_bytes))


def _dot(a, b):
    return jnp.dot(a, b, preferred_element_type=_F32)


def _sub_tiles(rows):
    return [slice(r0, r0 + SUB_TILE) for r0 in range(0, rows, SUB_TILE)]


def _rms_unit(x):
    ms = jnp.mean(x * x, axis=-1, keepdims=True)
    return x * lax.rsqrt(ms + EPS)


def _gain_spec(layer):
    return pl.BlockSpec((1, 1, D_MODEL), lambda *_: (layer, 0, 0))


def _stage_rows(rows, cols):
    best = V7X_SUBLANES
    for cand in range(V7X_SUBLANES, rows + 1, V7X_SUBLANES):
        if rows % cand == 0 and cand * cols * 4 <= STAGE_CHUNK_BYTES:
            best = cand
    return best


def _stage_weights(first_step, jobs):
    @pl.when(first_step)
    def _():
        for src, dst in jobs:
            rows, cols = dst.shape
            chunk = _stage_rows(rows, cols)
            n_chunks = rows // chunk

            def load(stage, sem):
                def copy(c):
                    return pltpu.make_async_copy(
                        src.at[pl.ds(c * chunk, chunk)], stage.at[c % 2], sem.at[c % 2])

                copy(0).start()
                for c in range(n_chunks):
                    if c + 1 < n_chunks:
                        copy(c + 1).start()
                    copy(c).wait()
                    dst[c * chunk:(c + 1) * chunk, :] = stage[c % 2].astype(_BF16)

            pl.run_scoped(load, pltpu.VMEM((2, chunk, cols), _F32), pltpu.SemaphoreType.DMA((2,)))


def _memkv_kernel(mem_ref, g_ref, w_hbm, kv_ref, w_ref):
    _stage_weights(pl.program_id(1) == 0, [(w_hbm.at[pl.program_id(0)], w_ref)])
    h = (_rms_unit(mem_ref[0]) * g_ref[0]).astype(_BF16)
    kv_ref[0, 0] = _dot(h, w_ref[...]).astype(_BF16)


def _memkv(mem, norm_memkv, w_kv):
    bsz = mem.shape[0]
    blk = MEM_LEN * D_MODEL * 4 + MEM_LEN * 2 * D_MODEL * 2
    resident = D_MODEL * 2 * D_MODEL * 2 + 2 * STAGE_CHUNK_BYTES + 4 * MEM_LEN * 2 * D_MODEL * 4
    return pl.pallas_call(
        _memkv_kernel,
        grid=(DEPTH, bsz),
        in_specs=[
            pl.BlockSpec((1, MEM_LEN, D_MODEL), lambda l, b: (b, 0, 0)),
            pl.BlockSpec((1, 1, D_MODEL), lambda l, b: (l, 0, 0)),
            _HBM,
        ],
        out_specs=pl.BlockSpec((1, 1, MEM_LEN, 2 * D_MODEL), lambda l, b: (l, b, 0, 0)),
        out_shape=jax.ShapeDtypeStruct((DEPTH, bsz, MEM_LEN, 2 * D_MODEL), _BF16),
        scratch_shapes=[pltpu.VMEM((D_MODEL, 2 * D_MODEL), _BF16)],
        compiler_params=_params(blk, resident, 2),
        name="memkv",
    )(mem, norm_memkv.reshape(DEPTH, 1, D_MODEL), w_kv)


def _shift_rows(z, prev, k):
    zk = pltpu.roll(z, k, 0)
    pk = pltpu.roll(prev, k, 0)
    r = lax.broadcasted_iota(jnp.int32, prev.shape, 0)
    top = jnp.where(r < k, pk, zk[:V7X_SUBLANES])
    return jnp.concatenate([top, zk[V7X_SUBLANES:]], axis=0)


def _convmix_kernel(x_ref, g_ref, win_hbm, cw_ref, wout_hbm, o_ref, win_ref, wout_ref, carry_ref,
                    *, tiles_per_seq):
    d = D_MODEL
    _stage_weights(pl.program_id(0) == 0, [(win_hbm.at[0], win_ref), (wout_hbm.at[0], wout_ref)])

    @pl.when(pl.program_id(0) % tiles_per_seq == 0)
    def _():
        carry_ref[...] = jnp.zeros_like(carry_ref)

    for rows in _sub_tiles(x_ref.shape[0]):
        x = x_ref[rows, :]
        h = (_rms_unit(x) * g_ref[0]).astype(_BF16)
        b_gate = _dot(h, win_ref[:, 0:d])
        z = _dot(h, win_ref[:, d:2 * d]) * _dot(h, win_ref[:, 2 * d:3 * d])
        prev = carry_ref[...]
        carry_ref[...] = z[z.shape[0] - V7X_SUBLANES:, :]
        conv = (cw_ref[0, 2:3, :] * z
                + cw_ref[0, 1:2, :] * _shift_rows(z, prev, 1)
                + cw_ref[0, 0:1, :] * _shift_rows(z, prev, 2))
        y = (b_gate * conv).astype(_BF16)
        o_ref[rows, :] = x + _dot(y, wout_ref[...])


def _convmix(x2d, norm_mix, w_in, w_conv, w_out, seq):
    n = x2d.shape[0]
    tm = TOKEN_TILE
    blk = 2 * tm * D_MODEL * 4
    resident = 4 * D_MODEL * D_MODEL * 2 + 2 * STAGE_CHUNK_BYTES + 10 * SUB_TILE * D_MODEL * 4
    return pl.pallas_call(
        functools.partial(_convmix_kernel, tiles_per_seq=seq // tm),
        grid=(n // tm,),
        in_specs=[
            pl.BlockSpec((tm, D_MODEL), lambda i: (i, 0)),
            _gain_spec(0),
            _HBM,
            pl.BlockSpec((1, CONV_W, D_MODEL), lambda i: (0, 0, 0)),
            _HBM,
        ],
        out_specs=pl.BlockSpec((tm, D_MODEL), lambda i: (i, 0)),
        out_shape=jax.ShapeDtypeStruct((n, D_MODEL), _F32),
        scratch_shapes=[
            pltpu.VMEM((D_MODEL, 3 * D_MODEL), _BF16),
            pltpu.VMEM((D_MODEL, D_MODEL), _BF16),
            pltpu.VMEM((V7X_SUBLANES, D_MODEL), _F32),
        ],
        compiler_params=_params(blk, resident, 1),
        name="convmix",
    )(x2d, norm_mix.reshape(DEPTH, 1, D_MODEL), w_in, w_conv, w_out)


def _memattn_kernel(*refs, has_pre, layer):
    if has_pre:
        x_ref, a_ref, wpre_hbm, g_ref, wq_hbm, kv_ref, wo_hbm, o_ref, wpre_ref, wq_ref, wo_ref = refs
        jobs = [(wpre_hbm.at[0], wpre_ref)]
    else:
        x_ref, g_ref, wq_hbm, kv_ref, wo_hbm, o_ref, wq_ref, wo_ref = refs
        jobs = []
    _stage_weights(pl.program_id(0) == 0, jobs + [(wq_hbm.at[layer], wq_ref), (wo_hbm.at[layer], wo_ref)])
    for rows in _sub_tiles(x_ref.shape[0]):
        x = x_ref[rows, :]
        if has_pre:
            x = x + _dot(a_ref[rows, :], wpre_ref[...])
        h = (_rms_unit(x) * g_ref[0]).astype(_BF16)
        q = (_dot(h, wq_ref[...]) * (MEM_HEAD_DIM ** -0.5)).astype(_BF16)
        heads = []
        for hd in range(MEM_HEADS):
            c0 = hd * MEM_HEAD_DIM
            k_h = kv_ref[0, 0, :, c0:c0 + MEM_HEAD_DIM]
            v_h = kv_ref[0, 0, :, D_MODEL + c0:D_MODEL + c0 + MEM_HEAD_DIM]
            s = lax.dot_general(q[:, c0:c0 + MEM_HEAD_DIM], k_h, _NT, preferred_element_type=_F32)
            p = jnp.exp(s - jnp.max(s, axis=-1, keepdims=True))
            inv_l = 1.0 / jnp.sum(p, axis=-1, keepdims=True)
            heads.append((_dot(p.astype(_BF16), v_h) * inv_l).astype(_BF16))
        attn = jnp.concatenate(heads, axis=1)
        o_ref[rows, :] = x + _dot(attn, wo_ref[...])


def _memattn(x2d, norm_mem, w_q, kv, w_o, layer, seq, pre=None):
    n = x2d.shape[0]
    tm = TOKEN_TILE
    tiles_per_seq = seq // tm
    row_spec = pl.BlockSpec((tm, D_MODEL), lambda i: (i, 0))
    w_scratch = pltpu.VMEM((D_MODEL, D_MODEL), _BF16)
    in_specs = [row_spec]
    args = [x2d]
    scratch = [w_scratch, w_scratch]
    if pre is not None:
        in_specs += [row_spec, _HBM]
        args += list(pre)
        scratch.append(w_scratch)
    in_specs += [
        _gain_spec(layer),
        _HBM,
        pl.BlockSpec((1, 1, MEM_LEN, 2 * D_MODEL), lambda i: (layer, i // tiles_per_seq, 0, 0)),
        _HBM,
    ]
    args += [norm_mem.reshape(DEPTH, 1, D_MODEL), w_q, kv, w_o]
    blk = 3 * tm * D_MODEL * 4 + MEM_LEN * 2 * D_MODEL * 2
    resident = len(scratch) * D_MODEL * D_MODEL * 2 + 2 * STAGE_CHUNK_BYTES + 8 * SUB_TILE * D_MODEL * 4
    return pl.pallas_call(
        functools.partial(_memattn_kernel, has_pre=pre is not None, layer=layer),
        grid=(n // tm,),
        in_specs=in_specs,
        out_specs=row_spec,
        out_shape=jax.ShapeDtypeStruct((n, D_MODEL), _F32),
        scratch_shapes=scratch,
        compiler_params=_params(blk, resident, 1),
        name="memattn_l%d" % layer,
    )(*args)


def _ff_chunks():
    return [(c0, min(FF_CHUNK, D_FF - c0)) for c0 in range(0, D_FF, FF_CHUNK)]


def _swiglu_kernel(*refs, final, layer):
    if final:
        x_ref, g_ref, wgu_hbm, wd_hbm, gf_ref, o_ref, wgu_ref, wd_ref = refs
    else:
        x_ref, g_ref, wgu_hbm, wd_hbm, o_ref, wgu_ref, wd_ref = refs
    _stage_weights(pl.program_id(0) == 0, [(wgu_hbm.at[layer], wgu_ref), (wd_hbm.at[layer], wd_ref)])
    for rows in _sub_tiles(x_ref.shape[0]):
        x = x_ref[rows, :]
        h = (_rms_unit(x) * g_ref[0]).astype(_BF16)
        acc = x
        for c0, cw in _ff_chunks():
            gate = _dot(h, wgu_ref[:, c0:c0 + cw])
            up = _dot(h, wgu_ref[:, D_FF + c0:D_FF + c0 + cw])
            act = (gate * (1.0 / (1.0 + jnp.exp(-gate))) * up).astype(_BF16)
            acc = acc + _dot(act, wd_ref[c0:c0 + cw, :])
        if final:
            acc = _rms_unit(acc) * gf_ref[0]
        o_ref[rows, :] = acc


def _swiglu(x2d, norm_ffn, w_gu, w_down, layer, final_gain=None):
    n = x2d.shape[0]
    tm = TOKEN_TILE
    in_specs = [pl.BlockSpec((tm, D_MODEL), lambda i: (i, 0)), _gain_spec(layer), _HBM, _HBM]
    args = [x2d, norm_ffn.reshape(DEPTH, 1, D_MODEL), w_gu, w_down]
    if final_gain is not None:
        in_specs.append(_gain_spec(0))
        args.append(final_gain.reshape(1, 1, D_MODEL))
    blk = 2 * tm * D_MODEL * 4
    resident = (3 * D_MODEL * D_FF * 2 + 2 * STAGE_CHUNK_BYTES
                + 4 * SUB_TILE * D_MODEL * 4 + 4 * SUB_TILE * FF_CHUNK * 4)
    return pl.pallas_call(
        functools.partial(_swiglu_kernel, final=final_gain is not None, layer=layer),
        grid=(n // tm,),
        in_specs=in_specs,
        out_specs=pl.BlockSpec((tm, D_MODEL), lambda i: (i, 0)),
        out_shape=jax.ShapeDtypeStruct((n, D_MODEL), _F32),
        scratch_shapes=[pltpu.VMEM((D_MODEL, 2 * D_FF), _BF16), pltpu.VMEM((D_FF, D_MODEL), _BF16)],
        compiler_params=_params(blk, resident, 1),
        name="swiglu_final" if final_gain is not None else "swiglu",
    )(*args)


def _rope(x, cos, sin_signed, first_half):
    d = x.shape[1]
    rot = jnp.where(first_half, pltpu.roll(x, d - ROT_DIM // 2, 1), pltpu.roll(x, ROT_DIM // 2, 1))
    return x * cos + rot * sin_signed


def _qkv_kernel(x_ref, pos_ref, invf_ref, gq_ref, gk_ref, wq_hbm, wkv_hbm,
                q_ref, k_ref, vt_ref, km_ref, wq_ref, wkv_ref, *, tiles_per_seq):
    d = D_MODEL
    tm = x_ref.shape[0]
    blocks_per_tile = tm // BLOCK
    s_tile = pl.program_id(0) % tiles_per_seq
    _stage_weights(pl.program_id(0) == 0, [(wq_hbm.at[0], wq_ref), (wkv_hbm, wkv_ref)])

    @pl.when(s_tile == 0)
    def _():
        km_ref[...] = jnp.zeros_like(km_ref)

    lane = lax.broadcasted_iota(jnp.int32, (1, HEAD_DIM), 1)
    lane_d = lax.broadcasted_iota(jnp.int32, (1, d), 1)
    first_half = (lane_d % HEAD_DIM) < ROT_DIM // 2
    row16 = lax.broadcasted_iota(jnp.int32, km_ref.shape[1:], 0)

    for rows in _sub_tiles(tm):
        xh = _rms_unit(x_ref[rows, :])
        hq = (xh * gq_ref[0]).astype(_BF16)
        hk = (xh * gk_ref[0]).astype(_BF16)

        pos = pos_ref[0, :, rows].astype(_F32)
        pos_rows = jnp.broadcast_to(pos, (V7X_LANES, SUB_TILE)).T
        ang = pos_rows * invf_ref[...]
        cos = jnp.cos(ang)
        sin_signed = jnp.where(lane < ROT_DIM // 2, -1.0, 1.0) * jnp.sin(ang)
        cos = jnp.concatenate([cos] * N_HEADS, axis=1)
        sin_signed = jnp.concatenate([sin_signed] * N_HEADS, axis=1)

        q = _dot(hq, wq_ref[...]) * (HEAD_DIM ** -0.5 * math.log2(math.e))
        q_ref[rows, :] = _rope(q, cos, sin_signed, first_half).astype(_BF16)

        k = _rope(_dot(hk, wkv_ref[:, 0:d]), cos, sin_signed, first_half)
        k_ref[rows, :] = k.astype(_BF16)
        v = _dot(hk, wkv_ref[:, d:2 * d])

        for r0 in range(0, SUB_TILE, BLOCK):
            j = (rows.start + r0) // BLOCK
            vt_ref[0, j] = v[r0:r0 + BLOCK, :].T.astype(_BF16)
            mean_row = jnp.sum(k[r0:r0 + BLOCK, :], axis=0, keepdims=True) * (1.0 / BLOCK)
            km_ref[0] = jnp.where(row16 == s_tile * blocks_per_tile + j, mean_row, km_ref[0])


def _qkv(x2d, positions, norm_mix, kv_norm, w_q, w_kv, bsz, seq):
    n = x2d.shape[0]
    tm = QKV_TOKEN_TILE
    tiles_per_seq = seq // tm
    n_blocks = seq // BLOCK
    inv_freq = ROPE_THETA ** (-jnp.arange(0, ROT_DIM, 2, dtype=_F32) / ROT_DIM)
    invf_lane = jnp.concatenate(
        [inv_freq, inv_freq, jnp.zeros((HEAD_DIM - ROT_DIM,), _F32)]).reshape(1, HEAD_DIM)
    row_spec = pl.BlockSpec((tm, D_MODEL), lambda i: (i, 0))
    blk = tm * D_MODEL * 4 + 3 * tm * D_MODEL * 2 + n_blocks * D_MODEL * 4
    resident = 3 * D_MODEL * D_MODEL * 2 + 2 * STAGE_CHUNK_BYTES + 12 * SUB_TILE * D_MODEL * 4
    return pl.pallas_call(
        functools.partial(_qkv_kernel, tiles_per_seq=tiles_per_seq),
        grid=(n // tm,),
        in_specs=[
            row_spec,
            pl.BlockSpec((1, 1, tm), lambda i: (i, 0, 0)),
            pl.BlockSpec((1, HEAD_DIM), lambda i: (0, 0)),
            _gain_spec(1),
            _gain_spec(0),
            _HBM,
            _HBM,
        ],
        out_specs=[
            row_spec,
            row_spec,
            pl.BlockSpec((1, tm // BLOCK, D_MODEL, BLOCK),
                         lambda i: (i // tiles_per_seq, i % tiles_per_seq, 0, 0)),
            pl.BlockSpec((1, n_blocks, D_MODEL), lambda i: (i // tiles_per_seq, 0, 0)),
        ],
        out_shape=[
            jax.ShapeDtypeStruct((n, D_MODEL), _BF16),
            jax.ShapeDtypeStruct((n, D_MODEL), _BF16),
            jax.ShapeDtypeStruct((bsz, n_blocks, D_MODEL, BLOCK), _BF16),
            jax.ShapeDtypeStruct((bsz, n_blocks, D_MODEL), _F32),
        ],
        scratch_shapes=[pltpu.VMEM((D_MODEL, D_MODEL), _BF16), pltpu.VMEM((D_MODEL, 2 * D_MODEL), _BF16)],
        compiler_params=_params(blk, resident, 1),
        name="qkv",
    )(x2d, positions.reshape(n // tm, 1, tm), invf_lane,
      norm_mix.reshape(DEPTH, 1, D_MODEL), kv_norm.reshape(1, 1, D_MODEL), w_q, w_kv)


def _moba_kernel(q_ref, k_ref, vt_ref, km_ref, o_ref, bias_ref, m_ref, l_ref, acc_ref):
    t = pl.program_id(1)
    n_blocks = km_ref.shape[2]
    own_rows = pl.ds(pl.multiple_of(t * BLOCK, BLOCK), BLOCK)
    chains = [(a, slice(h * HEAD_DIM, (h + 1) * HEAD_DIM))
              for a in range(q_ref.shape[0]) for h in range(N_HEADS)]

    def score_t(i, rows):
        a, c = chains[i]
        return lax.dot_general(k_ref[a, 0, rows, c], q_ref[a, 0, :, c], _NT, preferred_element_type=_F32)

    gates = []
    for a, c in chains:
        km = km_ref[a, 0, :, c]
        km_hi = km.astype(_BF16)
        km_lo = (km - km_hi.astype(_F32)).astype(_BF16)
        gates.append(lax.dot_general(km_hi, q_ref[a, 0, :, c], _NT, preferred_element_type=_F32)
                     + lax.dot_general(km_lo, q_ref[a, 0, :, c], _NT, preferred_element_type=_F32))
    own_scores = [score_t(i, own_rows) for i in range(len(chains))]

    blk = lax.broadcasted_iota(jnp.int32, (n_blocks, BLOCK), 0)
    for i, gate in enumerate(gates):
        rank = jnp.zeros(gate.shape, _F32)
        for m in range(n_blocks):
            g_m = gate[m:m + 1, :]
            beats = jnp.where(g_m > gate, 1.0, jnp.where(g_m == gate, jnp.where(blk > m, 1.0, 0.0), 0.0))
            rank = rank + jnp.where(m < t, beats, 0.0)
        selected = jnp.where(blk < t, rank, float(TOPK)) < TOPK
        bias_ref[i] = jnp.where(selected, 0.0, -jnp.inf)

    kpos = lax.broadcasted_iota(jnp.int32, (BLOCK, BLOCK), 0)
    qpos = lax.broadcasted_iota(jnp.int32, (BLOCK, BLOCK), 1)
    own_probs = []
    for i in range(len(chains)):
        s = jnp.where(kpos <= qpos, own_scores[i], -jnp.inf)
        m0 = jnp.max(s, axis=0, keepdims=True)
        p = jnp.exp2(s - m0)
        m_ref[i:i + 1, :] = m0
        l_ref[i:i + 1, :] = jnp.sum(p, axis=0, keepdims=True)
        own_probs.append(p.astype(_BF16))
    for i, (a, c) in enumerate(chains):
        acc_ref[i] = _dot(vt_ref[a, 0, t, c, :], own_probs[i])

    ones_rows = jnp.ones((_BF16_SUBLANES, BLOCK), _BF16)

    def past_block_pair(j, carry):
        n0 = 2 * j
        rows = pl.ds(pl.multiple_of(n0 * BLOCK, 2 * BLOCK), 2 * BLOCK)
        scores = {i: score_t(i, rows) for i in range(MOBA_SCORE_LOOKAHEAD)}
        for i, (a, c) in enumerate(chains):
            s = scores.pop(i)
            s_a, s_b = s[:BLOCK], s[BLOCK:]
            bias_a = bias_ref[i, pl.ds(n0, 1), :]
            bias_b = bias_ref[i, pl.ds(n0 + 1, 1), :]
            m_run = m_ref[i:i + 1, :]
            m_new = jnp.maximum(m_run, jnp.maximum(jnp.max(s_a, axis=0, keepdims=True) + bias_a,
                                                   jnp.max(s_b, axis=0, keepdims=True) + bias_b))
            m_ref[i:i + 1, :] = m_new
            alpha = jnp.exp2(m_run - m_new)
            p_a = jnp.exp2(s_a - (m_new - bias_a)).astype(_BF16)
            p_b = jnp.exp2(s_b - (m_new - bias_b)).astype(_BF16)
            vt_a = jnp.concatenate([vt_ref[a, 0, n0, c, :], ones_rows], axis=0)
            vt_b = jnp.concatenate([vt_ref[a, 0, n0 + 1, c, :], ones_rows], axis=0)
            pv = _dot(vt_a, p_a) + _dot(vt_b, p_b)
            if i + MOBA_SCORE_LOOKAHEAD < len(chains):
                scores[i + MOBA_SCORE_LOOKAHEAD] = score_t(i + MOBA_SCORE_LOOKAHEAD, rows)
            acc_ref[i] = alpha * acc_ref[i] + pv[:HEAD_DIM]
            l_ref[i:i + 1, :] = alpha * l_ref[i:i + 1, :] + pv[HEAD_DIM:HEAD_DIM + 1]
        return carry

    lax.fori_loop(0, lax.shift_right_logical(t + 1, 1), past_block_pair, 0)
    for i, (a, c) in enumerate(chains):
        out = acc_ref[i] * (1.0 / l_ref[i:i + 1, :])
        o_ref[a, 0, :, c] = out.T.astype(_BF16)


def _moba(q, k, vt, k_mean, bsz, seq):
    n_blocks = seq // BLOCK
    bps = MOBA_BATCHES_PER_STEP
    groups = bsz // bps
    n_chains = bps * N_HEADS
    q4 = q.reshape(bps, groups, seq, D_MODEL)
    k4 = k.reshape(bps, groups, seq, D_MODEL)
    vt5 = vt.reshape(bps, groups, n_blocks, D_MODEL, BLOCK)
    km4 = k_mean.reshape(bps, groups, n_blocks, D_MODEL)
    single = pl.Buffered(1)
    blk = bps * (2 * BLOCK * D_MODEL * 2 + seq * D_MODEL * 2 + n_blocks * D_MODEL * 4)
    scratch = n_chains * (n_blocks + HEAD_DIM + 2) * BLOCK * 4
    out = pl.pallas_call(
        _moba_kernel,
        grid=(groups, n_blocks),
        in_specs=[
            pl.BlockSpec((bps, 1, BLOCK, D_MODEL), lambda g, t: (0, g, t, 0)),
            pl.BlockSpec((bps, 1, seq, D_MODEL), lambda g, t: (0, g, 0, 0), pipeline_mode=single),
            pl.BlockSpec((bps, 1, n_blocks, D_MODEL, BLOCK), lambda g, t: (0, g, 0, 0, 0),
                         pipeline_mode=single),
            pl.BlockSpec((bps, 1, n_blocks, D_MODEL), lambda g, t: (0, g, 0, 0)),
        ],
        out_specs=pl.BlockSpec((bps, 1, BLOCK, D_MODEL), lambda g, t: (0, g, t, 0)),
        out_shape=jax.ShapeDtypeStruct((bps, groups, seq, D_MODEL), _BF16),
        scratch_shapes=[
            pltpu.VMEM((n_chains, n_blocks, BLOCK), _F32),
            pltpu.VMEM((n_chains, BLOCK), _F32),
            pltpu.VMEM((n_chains, BLOCK), _F32),
            pltpu.VMEM((n_chains, HEAD_DIM, BLOCK), _F32),
        ],
        compiler_params=_params(blk, scratch + n_chains * 6 * BLOCK * BLOCK * 4, 2),
        name="moba",
    )(q4, k4, vt5, km4)
    return out.reshape(bsz * seq, D_MODEL)


def kernel(x, mem, positions, norm_mix, norm_mem, norm_memkv, norm_ffn, norm_final,
           conv_w_in, conv_w, conv_w_out, kv_norm, w_kv, moba_w_q, moba_w_o,
           mem_w_q, mem_w_kv, mem_w_o, ffn_w_gu, ffn_w_down):
    bsz, seq, d = x.shape
    assert d == D_MODEL and seq % TOKEN_TILE == 0 and seq % QKV_TOKEN_TILE == 0
    assert TOKEN_TILE % SUB_TILE == 0 and QKV_TOKEN_TILE % SUB_TILE == 0 and SUB_TILE % BLOCK == 0
    assert mem.shape == (bsz, MEM_LEN, D_MODEL)
    assert bsz % MOBA_BATCHES_PER_STEP == 0
    assert norm_mix.shape[0] == DEPTH == 2 and conv_w_in.shape[0] == 1 and moba_w_q.shape[0] == 1

    x2d = x.reshape(bsz * seq, d)
    kv_mem = _memkv(mem, norm_memkv, mem_w_kv)

    x2d = _convmix(x2d, norm_mix, conv_w_in, conv_w, conv_w_out, seq)
    x2d = _memattn(x2d, norm_mem, mem_w_q, kv_mem, mem_w_o, 0, seq)
    x2d = _swiglu(x2d, norm_ffn, ffn_w_gu, ffn_w_down, 0)

    q, k, vt, k_mean = _qkv(x2d, positions, norm_mix, kv_norm, moba_w_q, w_kv, bsz, seq)
    attn = _moba(q, k, vt, k_mean, bsz, seq)

    x2d = _memattn(x2d, norm_mem, mem_w_q, kv_mem, mem_w_o, 1, seq, pre=(attn, moba_w_o))
    x2d = _swiglu(x2d, norm_ffn, ffn_w_gu, ffn_w_down, 1, final_gain=norm_final)
    return x2d.reshape(bsz, seq, d)
```

```python
import functools
import math

import jax
import jax.numpy as jnp
from jax import lax
from jax.experimental import pallas as pl
from jax.experimental.pallas import tpu as pltpu

D_MODEL = 1024
DEPTH = 2
CONV_W = 3
N_HEADS = 8
HEAD_DIM = D_MODEL // N_HEADS
BLOCK = 256
TOPK = 3
ROT_DIM = HEAD_DIM // 4
ROPE_THETA = 500000.0
MEM_LEN = 256
MEM_HEADS = 4
MEM_HEAD_DIM = D_MODEL // MEM_HEADS
D_FF = ((8 * D_MODEL + 3 * 256 - 1) // (3 * 256)) * 256
EPS = 1e-6

V7X_VMEM_BYTES = 64 * 1024 * 1024
V7X_LANES = 128
V7X_SUBLANES = 8
V7X_MXU_DIM = 256
_BF16_SUBLANES = 2 * V7X_SUBLANES

TOKEN_TILE = 1024
QKV_TOKEN_TILE = 512
SUB_TILE = 512
FFN_SUB_TILE = 256
FF_CHUNK = 4 * V7X_MXU_DIM
MOBA_BATCHES_PER_STEP = 2
MOBA_SCORE_LOOKAHEAD = 4
STAGE_CHUNK_BYTES = 3 * 1024 * 1024

_F32 = jnp.float32
_BF16 = jnp.bfloat16
_NT = (((1,), (1,)), ((), ()))
_HBM = pl.BlockSpec(memory_space=pl.ANY)


def _vmem_limit(block_bytes, resident_bytes):
    need = 2 * block_bytes + resident_bytes
    return int(min(need + need // 4, V7X_VMEM_BYTES - 8 * 1024 * 1024))


def _params(block_bytes, resident_bytes, ndim):
    return pltpu.CompilerParams(
        dimension_semantics=("arbitrary",) * ndim,
        vmem_limit_bytes=_vmem_limit(block_bytes, resident_bytes))


def _dot(a, b):
    return jnp.dot(a, b, preferred_element_type=_F32)


def _sub_tiles(rows, sub=None):
    sub = sub or SUB_TILE
    return [slice(r0, r0 + sub) for r0 in range(0, rows, sub)]


def _rms_unit(x):
    ms = jnp.mean(x * x, axis=-1, keepdims=True)
    return x * lax.rsqrt(ms + EPS)


def _gain_spec(layer):
    return pl.BlockSpec((1, 1, D_MODEL), lambda *_: (layer, 0, 0))


def _stage_rows(rows, cols):
    best = V7X_SUBLANES
    for cand in range(V7X_SUBLANES, rows + 1, V7X_SUBLANES):
        if rows % cand == 0 and cand * cols * 4 <= STAGE_CHUNK_BYTES:
            best = cand
    return best


def _stage_weights(first_step, jobs):
    @pl.when(first_step)
    def _():
        for src, dst in jobs:
            rows, cols = dst.shape
            chunk = _stage_rows(rows, cols)
            n_chunks = rows // chunk

            def load(stage, sem):
                def copy(c):
                    return pltpu.make_async_copy(
                        src.at[pl.ds(c * chunk, chunk)], stage.at[c % 2], sem.at[c % 2])

                copy(0).start()
                for c in range(n_chunks):
                    if c + 1 < n_chunks:
                        copy(c + 1).start()
                    copy(c).wait()
                    dst[c * chunk:(c + 1) * chunk, :] = stage[c % 2].astype(_BF16)

            pl.run_scoped(load, pltpu.VMEM((2, chunk, cols), _F32), pltpu.SemaphoreType.DMA((2,)))


def _memkv_kernel(mem_ref, g_ref, w_hbm, kv_ref, w_ref):
    _stage_weights(pl.program_id(1) == 0, [(w_hbm.at[pl.program_id(0)], w_ref)])
    h = (_rms_unit(mem_ref[0]) * g_ref[0]).astype(_BF16)
    kv_ref[0, 0] = _dot(h, w_ref[...]).astype(_BF16)


def _memkv(mem, norm_memkv, w_kv):
    bsz = mem.shape[0]
    blk = MEM_LEN * D_MODEL * 4 + MEM_LEN * 2 * D_MODEL * 2
    resident = D_MODEL * 2 * D_MODEL * 2 + 2 * STAGE_CHUNK_BYTES + 4 * MEM_LEN * 2 * D_MODEL * 4
    return pl.pallas_call(
        _memkv_kernel,
        grid=(DEPTH, bsz),
        in_specs=[
            pl.BlockSpec((1, MEM_LEN, D_MODEL), lambda l, b: (b, 0, 0)),
            pl.BlockSpec((1, 1, D_MODEL), lambda l, b: (l, 0, 0)),
            _HBM,
        ],
        out_specs=pl.BlockSpec((1, 1, MEM_LEN, 2 * D_MODEL), lambda l, b: (l, b, 0, 0)),
        out_shape=jax.ShapeDtypeStruct((DEPTH, bsz, MEM_LEN, 2 * D_MODEL), _BF16),
        scratch_shapes=[pltpu.VMEM((D_MODEL, 2 * D_MODEL), _BF16)],
        compiler_params=_params(blk, resident, 2),
        name="memkv",
    )(mem, norm_memkv.reshape(DEPTH, 1, D_MODEL), w_kv)


def _shift_rows(z, prev, k):
    zk = pltpu.roll(z, k, 0)
    pk = pltpu.roll(prev, k, 0)
    r = lax.broadcasted_iota(jnp.int32, prev.shape, 0)
    top = jnp.where(r < k, pk, zk[:V7X_SUBLANES])
    return jnp.concatenate([top, zk[V7X_SUBLANES:]], axis=0)


def _convmix_kernel(x_ref, g_ref, win_hbm, cw_ref, wout_hbm, o_ref, win_ref, wout_ref, carry_ref,
                    *, tiles_per_seq):
    d = D_MODEL
    _stage_weights(pl.program_id(0) == 0, [(win_hbm.at[0], win_ref), (wout_hbm.at[0], wout_ref)])

    @pl.when(pl.program_id(0) % tiles_per_seq == 0)
    def _():
        carry_ref[...] = jnp.zeros_like(carry_ref)

    for rows in _sub_tiles(x_ref.shape[0]):
        x = x_ref[rows, :]
        h = (_rms_unit(x) * g_ref[0]).astype(_BF16)
        b_gate = _dot(h, win_ref[:, 0:d])
        z = _dot(h, win_ref[:, d:2 * d]) * _dot(h, win_ref[:, 2 * d:3 * d])
        prev = carry_ref[...]
        carry_ref[...] = z[z.shape[0] - V7X_SUBLANES:, :]
        conv = (cw_ref[0, 2:3, :] * z
                + cw_ref[0, 1:2, :] * _shift_rows(z, prev, 1)
                + cw_ref[0, 0:1, :] * _shift_rows(z, prev, 2))
        y = (b_gate * conv).astype(_BF16)
        o_ref[rows, :] = x + _dot(y, wout_ref[...])


def _convmix(x2d, norm_mix, w_in, w_conv, w_out, seq):
    n = x2d.shape[0]
    tm = TOKEN_TILE
    blk = 2 * tm * D_MODEL * 4
    resident = 4 * D_MODEL * D_MODEL * 2 + 2 * STAGE_CHUNK_BYTES + 10 * SUB_TILE * D_MODEL * 4
    return pl.pallas_call(
        functools.partial(_convmix_kernel, tiles_per_seq=seq // tm),
        grid=(n // tm,),
        in_specs=[
            pl.BlockSpec((tm, D_MODEL), lambda i: (i, 0)),
            _gain_spec(0),
            _HBM,
            pl.BlockSpec((1, CONV_W, D_MODEL), lambda i: (0, 0, 0)),
            _HBM,
        ],
        out_specs=pl.BlockSpec((tm, D_MODEL), lambda i: (i, 0)),
        out_shape=jax.ShapeDtypeStruct((n, D_MODEL), _F32),
        scratch_shapes=[
            pltpu.VMEM((D_MODEL, 3 * D_MODEL), _BF16),
            pltpu.VMEM((D_MODEL, D_MODEL), _BF16),
            pltpu.VMEM((V7X_SUBLANES, D_MODEL), _F32),
        ],
        compiler_params=_params(blk, resident, 1),
        name="convmix",
    )(x2d, norm_mix.reshape(DEPTH, 1, D_MODEL), w_in, w_conv, w_out)


def _memattn_kernel(*refs, has_pre, layer):
    if has_pre:
        x_ref, a_ref, wpre_hbm, g_ref, wq_hbm, kv_ref, wo_hbm, o_ref, wpre_ref, wq_ref, wo_ref = refs
        jobs = [(wpre_hbm.at[0], wpre_ref)]
    else:
        x_ref, g_ref, wq_hbm, kv_ref, wo_hbm, o_ref, wq_ref, wo_ref = refs
        jobs = []
    _stage_weights(pl.program_id(0) == 0, jobs + [(wq_hbm.at[layer], wq_ref), (wo_hbm.at[layer], wo_ref)])
    for rows in _sub_tiles(x_ref.shape[0]):
        x = x_ref[rows, :]
        if has_pre:
            x = x + _dot(a_ref[rows, :], wpre_ref[...])
        h = (_rms_unit(x) * g_ref[0]).astype(_BF16)
        q = (_dot(h, wq_ref[...]) * (MEM_HEAD_DIM ** -0.5)).astype(_BF16)
        heads = []
        for hd in range(MEM_HEADS):
            c0 = hd * MEM_HEAD_DIM
            k_h = kv_ref[0, 0, :, c0:c0 + MEM_HEAD_DIM]
            v_h = kv_ref[0, 0, :, D_MODEL + c0:D_MODEL + c0 + MEM_HEAD_DIM]
            s = lax.dot_general(q[:, c0:c0 + MEM_HEAD_DIM], k_h, _NT, preferred_element_type=_F32)
            p = jnp.exp(s - jnp.max(s, axis=-1, keepdims=True))
            inv_l = 1.0 / jnp.sum(p, axis=-1, keepdims=True)
            heads.append((_dot(p.astype(_BF16), v_h) * inv_l).astype(_BF16))
        attn = jnp.concatenate(heads, axis=1)
        o_ref[rows, :] = x + _dot(attn, wo_ref[...])


def _memattn(x2d, norm_mem, w_q, kv, w_o, layer, seq, pre=None):
    n = x2d.shape[0]
    tm = TOKEN_TILE
    tiles_per_seq = seq // tm
    row_spec = pl.BlockSpec((tm, D_MODEL), lambda i: (i, 0))
    w_scratch = pltpu.VMEM((D_MODEL, D_MODEL), _BF16)
    in_specs = [row_spec]
    args = [x2d]
    scratch = [w_scratch, w_scratch]
    if pre is not None:
        in_specs += [row_spec, _HBM]
        args += list(pre)
        scratch.append(w_scratch)
    in_specs += [
        _gain_spec(layer),
        _HBM,
        pl.BlockSpec((1, 1, MEM_LEN, 2 * D_MODEL), lambda i: (layer, i // tiles_per_seq, 0, 0)),
        _HBM,
    ]
    args += [norm_mem.reshape(DEPTH, 1, D_MODEL), w_q, kv, w_o]
    blk = 3 * tm * D_MODEL * 4 + MEM_LEN * 2 * D_MODEL * 2
    resident = len(scratch) * D_MODEL * D_MODEL * 2 + 2 * STAGE_CHUNK_BYTES + 8 * SUB_TILE * D_MODEL * 4
    return pl.pallas_call(
        functools.partial(_memattn_kernel, has_pre=pre is not None, layer=layer),
        grid=(n // tm,),
        in_specs=in_specs,
        out_specs=row_spec,
        out_shape=jax.ShapeDtypeStruct((n, D_MODEL), _F32),
        scratch_shapes=scratch,
        compiler_params=_params(blk, resident, 1),
        name="memattn_l%d" % layer,
    )(*args)


def _ff_chunks():
    return [(c0, min(FF_CHUNK, D_FF - c0)) for c0 in range(0, D_FF, FF_CHUNK)]


def _swiglu_kernel(*refs, final, layer):
    if final:
        x_ref, g_ref, wgu_hbm, wd_hbm, gf_ref, o_ref, wgu_ref, wd_ref = refs
    else:
        x_ref, g_ref, wgu_hbm, wd_hbm, o_ref, wgu_ref, wd_ref = refs
    _stage_weights(pl.program_id(0) == 0, [(wgu_hbm.at[layer], wgu_ref), (wd_hbm.at[layer], wd_ref)])
    for rows in _sub_tiles(x_ref.shape[0], FFN_SUB_TILE):
        x = x_ref[rows, :]
        h = (_rms_unit(x) * g_ref[0]).astype(_BF16)
        acc = x
        for c0, cw in _ff_chunks():
            gate = _dot(h, wgu_ref[:, c0:c0 + cw])
            up = _dot(h, wgu_ref[:, D_FF + c0:D_FF + c0 + cw])
            act = (gate * (1.0 / (1.0 + jnp.exp(-gate))) * up).astype(_BF16)
            acc = acc + _dot(act, wd_ref[c0:c0 + cw, :])
        if final:
            acc = _rms_unit(acc) * gf_ref[0]
        o_ref[rows, :] = acc


def _swiglu(x2d, norm_ffn, w_gu, w_down, layer, final_gain=None):
    n = x2d.shape[0]
    tm = TOKEN_TILE
    in_specs = [pl.BlockSpec((tm, D_MODEL), lambda i: (i, 0)), _gain_spec(layer), _HBM, _HBM]
    args = [x2d, norm_ffn.reshape(DEPTH, 1, D_MODEL), w_gu, w_down]
    if final_gain is not None:
        in_specs.append(_gain_spec(0))
        args.append(final_gain.reshape(1, 1, D_MODEL))
    blk = 2 * tm * D_MODEL * 4
    resident = (3 * D_MODEL * D_FF * 2 + 2 * STAGE_CHUNK_BYTES
                + 4 * FFN_SUB_TILE * D_MODEL * 4 + 4 * FFN_SUB_TILE * FF_CHUNK * 4)
    return pl.pallas_call(
        functools.partial(_swiglu_kernel, final=final_gain is not None, layer=layer),
        grid=(n // tm,),
        in_specs=in_specs,
        out_specs=pl.BlockSpec((tm, D_MODEL), lambda i: (i, 0)),
        out_shape=jax.ShapeDtypeStruct((n, D_MODEL), _F32),
        scratch_shapes=[pltpu.VMEM((D_MODEL, 2 * D_FF), _BF16), pltpu.VMEM((D_FF, D_MODEL), _BF16)],
        compiler_params=_params(blk, resident, 1),
        name="swiglu_final" if final_gain is not None else "swiglu",
    )(*args)


def _rope(x, cos, sin_signed, first_half):
    d = x.shape[1]
    rot = jnp.where(first_half, pltpu.roll(x, d - ROT_DIM // 2, 1), pltpu.roll(x, ROT_DIM // 2, 1))
    return x * cos + rot * sin_signed


def _qkv_kernel(x_ref, pos_ref, invf_ref, gq_ref, gk_ref, wq_hbm, wkv_hbm,
                q_ref, k_ref, vt_ref, km_ref, wq_ref, wkv_ref, *, tiles_per_seq):
    d = D_MODEL
    tm = x_ref.shape[0]
    blocks_per_tile = tm // BLOCK
    s_tile = pl.program_id(0) % tiles_per_seq
    _stage_weights(pl.program_id(0) == 0, [(wq_hbm.at[0], wq_ref), (wkv_hbm, wkv_ref)])

    @pl.when(s_tile == 0)
    def _():
        km_ref[...] = jnp.zeros_like(km_ref)

    lane = lax.broadcasted_iota(jnp.int32, (1, HEAD_DIM), 1)
    lane_d = lax.broadcasted_iota(jnp.int32, (1, d), 1)
    first_half = (lane_d % HEAD_DIM) < ROT_DIM // 2
    row16 = lax.broadcasted_iota(jnp.int32, km_ref.shape[1:], 0)

    for rows in _sub_tiles(tm):
        xh = _rms_unit(x_ref[rows, :])
        hq = (xh * gq_ref[0]).astype(_BF16)
        hk = (xh * gk_ref[0]).astype(_BF16)

        pos = pos_ref[0, :, rows].astype(_F32)
        pos_rows = jnp.broadcast_to(pos, (V7X_LANES, SUB_TILE)).T
        ang = pos_rows * invf_ref[...]
        cos = jnp.cos(ang)
        sin_signed = jnp.where(lane < ROT_DIM // 2, -1.0, 1.0) * jnp.sin(ang)
        cos = jnp.concatenate([cos] * N_HEADS, axis=1)
        sin_signed = jnp.concatenate([sin_signed] * N_HEADS, axis=1)

        q = _dot(hq, wq_ref[...]) * (HEAD_DIM ** -0.5 * math.log2(math.e))
        q_ref[rows, :] = _rope(q, cos, sin_signed, first_half).astype(_BF16)

        k = _rope(_dot(hk, wkv_ref[:, 0:d]), cos, sin_signed, first_half)
        k_ref[rows, :] = k.astype(_BF16)
        v = _dot(hk, wkv_ref[:, d:2 * d])

        for r0 in range(0, SUB_TILE, BLOCK):
            j = (rows.start + r0) // BLOCK
            vt_ref[0, j] = v[r0:r0 + BLOCK, :].T.astype(_BF16)
            mean_row = jnp.sum(k[r0:r0 + BLOCK, :], axis=0, keepdims=True) * (1.0 / BLOCK)
            km_ref[0] = jnp.where(row16 == s_tile * blocks_per_tile + j, mean_row, km_ref[0])


def _qkv(x2d, positions, norm_mix, kv_norm, w_q, w_kv, bsz, seq):
    n = x2d.shape[0]
    tm = QKV_TOKEN_TILE
    tiles_per_seq = seq // tm
    n_blocks = seq // BLOCK
    inv_freq = ROPE_THETA ** (-jnp.arange(0, ROT_DIM, 2, dtype=_F32) / ROT_DIM)
    invf_lane = jnp.concatenate(
        [inv_freq, inv_freq, jnp.zeros((HEAD_DIM - ROT_DIM,), _F32)]).reshape(1, HEAD_DIM)
    row_spec = pl.BlockSpec((tm, D_MODEL), lambda i: (i, 0))
    blk = tm * D_MODEL * 4 + 3 * tm * D_MODEL * 2 + n_blocks * D_MODEL * 4
    resident = 3 * D_MODEL * D_MODEL * 2 + 2 * STAGE_CHUNK_BYTES + 12 * SUB_TILE * D_MODEL * 4
    return pl.pallas_call(
        functools.partial(_qkv_kernel, tiles_per_seq=tiles_per_seq),
        grid=(n // tm,),
        in_specs=[
            row_spec,
            pl.BlockSpec((1, 1, tm), lambda i: (i, 0, 0)),
            pl.BlockSpec((1, HEAD_DIM), lambda i: (0, 0)),
            _gain_spec(1),
            _gain_spec(0),
            _HBM,
            _HBM,
        ],
        out_specs=[
            row_spec,
            row_spec,
            pl.BlockSpec((1, tm // BLOCK, D_MODEL, BLOCK),
                         lambda i: (i // tiles_per_seq, i % tiles_per_seq, 0, 0)),
            pl.BlockSpec((1, n_blocks, D_MODEL), lambda i: (i // tiles_per_seq, 0, 0)),
        ],
        out_shape=[
            jax.ShapeDtypeStruct((n, D_MODEL), _BF16),
            jax.ShapeDtypeStruct((n, D_MODEL), _BF16),
            jax.ShapeDtypeStruct((bsz, n_blocks, D_MODEL, BLOCK), _BF16),
            jax.ShapeDtypeStruct((bsz, n_blocks, D_MODEL), _F32),
        ],
        scratch_shapes=[pltpu.VMEM((D_MODEL, D_MODEL), _BF16), pltpu.VMEM((D_MODEL, 2 * D_MODEL), _BF16)],
        compiler_params=_params(blk, resident, 1),
        name="qkv",
    )(x2d, positions.reshape(n // tm, 1, tm), invf_lane,
      norm_mix.reshape(DEPTH, 1, D_MODEL), kv_norm.reshape(1, 1, D_MODEL), w_q, w_kv)


def _moba_kernel(q_ref, k_ref, vt_ref, km_ref, o_ref, bias_ref, m_ref, l_ref, acc_ref):
    t = pl.program_id(1)
    n_blocks = km_ref.shape[2]
    own_rows = pl.ds(pl.multiple_of(t * BLOCK, BLOCK), BLOCK)
    chains = [(a, slice(h * HEAD_DIM, (h + 1) * HEAD_DIM))
              for a in range(q_ref.shape[0]) for h in range(N_HEADS)]

    def score_t(i, rows):
        a, c = chains[i]
        return lax.dot_general(k_ref[a, 0, rows, c], q_ref[a, 0, :, c], _NT, preferred_element_type=_F32)

    gates = []
    for a, c in chains:
        km = km_ref[a, 0, :, c]
        km_hi = km.astype(_BF16)
        km_lo = (km - km_hi.astype(_F32)).astype(_BF16)
        gates.append(lax.dot_general(km_hi, q_ref[a, 0, :, c], _NT, preferred_element_type=_F32)
                     + lax.dot_general(km_lo, q_ref[a, 0, :, c], _NT, preferred_element_type=_F32))
    own_scores = [score_t(i, own_rows) for i in range(len(chains))]

    blk = lax.broadcasted_iota(jnp.int32, (n_blocks, BLOCK), 0)
    for i, gate in enumerate(gates):
        rank = jnp.zeros(gate.shape, _F32)
        for m in range(n_blocks):
            g_m = gate[m:m + 1, :]
            beats = jnp.where(g_m > gate, 1.0, jnp.where(g_m == gate, jnp.where(blk > m, 1.0, 0.0), 0.0))
            rank = rank + jnp.where(m < t, beats, 0.0)
        selected = jnp.where(blk < t, rank, float(TOPK)) < TOPK
        bias_ref[i] = jnp.where(selected, 0.0, -jnp.inf)

    ones_rows = jnp.ones((_BF16_SUBLANES, BLOCK), _BF16)

    kpos = lax.broadcasted_iota(jnp.int32, (BLOCK, BLOCK), 0)
    qpos = lax.broadcasted_iota(jnp.int32, (BLOCK, BLOCK), 1)
    for i, (a, c) in enumerate(chains):
        s = jnp.where(kpos <= qpos, own_scores[i], -jnp.inf)
        m0 = jnp.max(s, axis=0, keepdims=True)
        p = jnp.exp2(s - m0).astype(_BF16)
        pv = _dot(jnp.concatenate([vt_ref[a, 0, t, c, :], ones_rows], axis=0), p)
        m_ref[i:i + 1, :] = m0
        l_ref[i:i + 1, :] = pv[HEAD_DIM:HEAD_DIM + 1]
        acc_ref[i] = pv[:HEAD_DIM]

    def past_block_pair(j, carry):
        n0 = 2 * j
        rows = pl.ds(pl.multiple_of(n0 * BLOCK, 2 * BLOCK), 2 * BLOCK)
        scores = {i: score_t(i, rows) for i in range(MOBA_SCORE_LOOKAHEAD)}
        for i, (a, c) in enumerate(chains):
            s = scores.pop(i)
            s_a, s_b = s[:BLOCK], s[BLOCK:]
            bias_a = bias_ref[i, pl.ds(n0, 1), :]
            bias_b = bias_ref[i, pl.ds(n0 + 1, 1), :]
            m_run = m_ref[i:i + 1, :]
            m_new = jnp.maximum(m_run, jnp.maximum(jnp.max(s_a, axis=0, keepdims=True) + bias_a,
                                                   jnp.max(s_b, axis=0, keepdims=True) + bias_b))
            m_ref[i:i + 1, :] = m_new
            alpha = jnp.exp2(m_run - m_new)
            p_a = jnp.exp2(s_a - (m_new - bias_a)).astype(_BF16)
            p_b = jnp.exp2(s_b - (m_new - bias_b)).astype(_BF16)
            vt_a = jnp.concatenate([vt_ref[a, 0, n0, c, :], ones_rows], axis=0)
            vt_b = jnp.concatenate([vt_ref[a, 0, n0 + 1, c, :], ones_rows], axis=0)
            pv = _dot(vt_a, p_a) + _dot(vt_b, p_b)
            if i + MOBA_SCORE_LOOKAHEAD < len(chains):
                scores[i + MOBA_SCORE_LOOKAHEAD] = score_t(i + MOBA_SCORE_LOOKAHEAD, rows)
            acc_ref[i] = alpha * acc_ref[i] + pv[:HEAD_DIM]
            l_ref[i:i + 1, :] = alpha * l_ref[i:i + 1, :] + pv[HEAD_DIM:HEAD_DIM + 1]
        return carry

    lax.fori_loop(0, lax.shift_right_logical(t + 1, 1), past_block_pair, 0)
    for i, (a, c) in enumerate(chains):
        out = acc_ref[i] * (1.0 / l_ref[i:i + 1, :])
        o_ref[a, 0, :, c] = out.T.astype(_BF16)


def _moba(q, k, vt, k_mean, bsz, seq):
    n_blocks = seq // BLOCK
    bps = MOBA_BATCHES_PER_STEP
    groups = bsz // bps
    n_chains = bps * N_HEADS
    q4 = q.reshape(bps, groups, seq, D_MODEL)
    k4 = k.reshape(bps, groups, seq, D_MODEL)
    vt5 = vt.reshape(bps, groups, n_blocks, D_MODEL, BLOCK)
    km4 = k_mean.reshape(bps, groups, n_blocks, D_MODEL)
    single = pl.Buffered(1)
    blk = bps * (2 * BLOCK * D_MODEL * 2 + seq * D_MODEL * 2 + n_blocks * D_MODEL * 4)
    scratch = n_chains * (n_blocks + HEAD_DIM + 2) * BLOCK * 4
    out = pl.pallas_call(
        _moba_kernel,
        grid=(groups, n_blocks),
        in_specs=[
            pl.BlockSpec((bps, 1, BLOCK, D_MODEL), lambda g, t: (0, g, t, 0)),
            pl.BlockSpec((bps, 1, seq, D_MODEL), lambda g, t: (0, g, 0, 0), pipeline_mode=single),
            pl.BlockSpec((bps, 1, n_blocks, D_MODEL, BLOCK), lambda g, t: (0, g, 0, 0, 0),
                         pipeline_mode=single),
            pl.BlockSpec((bps, 1, n_blocks, D_MODEL), lambda g, t: (0, g, 0, 0)),
        ],
        out_specs=pl.BlockSpec((bps, 1, BLOCK, D_MODEL), lambda g, t: (0, g, t, 0)),
        out_shape=jax.ShapeDtypeStruct((bps, groups, seq, D_MODEL), _BF16),
        scratch_shapes=[
            pltpu.VMEM((n_chains, n_blocks, BLOCK), _F32),
            pltpu.VMEM((n_chains, BLOCK), _F32),
            pltpu.VMEM((n_chains, BLOCK), _F32),
            pltpu.VMEM((n_chains, HEAD_DIM, BLOCK), _F32),
        ],
        compiler_params=_params(blk, scratch + n_chains * 6 * BLOCK * BLOCK * 4, 2),
        name="moba",
    )(q4, k4, vt5, km4)
    return out.reshape(bsz * seq, D_MODEL)


def kernel(x, mem, positions, norm_mix, norm_mem, norm_memkv, norm_ffn, norm_final,
           conv_w_in, conv_w, conv_w_out, kv_norm, w_kv, moba_w_q, moba_w_o,
           mem_w_q, mem_w_kv, mem_w_o, ffn_w_gu, ffn_w_down):
    bsz, seq, d = x.shape
    assert d == D_MODEL and seq % TOKEN_TILE == 0 and seq % QKV_TOKEN_TILE == 0
    assert TOKEN_TILE % SUB_TILE == 0 and QKV_TOKEN_TILE % SUB_TILE == 0 and SUB_TILE % BLOCK == 0
    assert TOKEN_TILE % FFN_SUB_TILE == 0
    assert mem.shape == (bsz, MEM_LEN, D_MODEL)
    assert bsz % MOBA_BATCHES_PER_STEP == 0
    assert norm_mix.shape[0] == DEPTH == 2 and conv_w_in.shape[0] == 1 and moba_w_q.shape[0] == 1

    x2d = x.reshape(bsz * seq, d)
    kv_mem = _memkv(mem, norm_memkv, mem_w_kv)

    x2d = _convmix(x2d, norm_mix, conv_w_in, conv_w, conv_w_out, seq)
    x2d = _memattn(x2d, norm_mem, mem_w_q, kv_mem, mem_w_o, 0, seq)
    x2d = _swiglu(x2d, norm_ffn, ffn_w_gu, ffn_w_down, 0)

    q, k, vt, k_mean = _qkv(x2d, positions, norm_mix, kv_norm, moba_w_q, w_kv, bsz, seq)
    attn = _moba(q, k, vt, k_mean, bsz, seq)

    x2d = _memattn(x2d, norm_mem, mem_w_q, kv_mem, mem_w_o, 1, seq, pre=(attn, moba_w_o))
    x2d = _swiglu(x2d, norm_ffn, ffn_w_gu, ffn_w_down, 1, final_gain=norm_final)
    return x2d.reshape(bsz, seq, d)
```

```python
import functools
import math

import jax
import jax.numpy as jnp
from jax import lax
from jax.experimental import pallas as pl
from jax.experimental.pallas import tpu as pltpu

D_MODEL = 1024
DEPTH = 2
CONV_W = 3
N_HEADS = 8
HEAD_DIM = D_MODEL // N_HEADS
BLOCK = 256
TOPK = 3
ROT_DIM = HEAD_DIM // 4
ROPE_THETA = 500000.0
MEM_LEN = 256
MEM_HEADS = 4
MEM_HEAD_DIM = D_MODEL // MEM_HEADS
D_FF = ((8 * D_MODEL + 3 * 256 - 1) // (3 * 256)) * 256
EPS = 1e-6

V7X_VMEM_BYTES = 64 * 1024 * 1024
V7X_LANES = 128
V7X_SUBLANES = 8
V7X_MXU_DIM = 256
_BF16_SUBLANES = 2 * V7X_SUBLANES

TOKEN_TILE = 1024
QKV_TOKEN_TILE = 512
SUB_TILE = 512
FFN_SUB_TILE = 256
FF_CHUNK = 4 * V7X_MXU_DIM
MOBA_BATCHES_PER_STEP = 2
MOBA_SCORE_LOOKAHEAD = 4
STAGE_CHUNK_BYTES = 3 * 1024 * 1024

_F32 = jnp.float32
_BF16 = jnp.bfloat16
_NT = (((1,), (1,)), ((), ()))
_HBM = pl.BlockSpec(memory_space=pl.ANY)


def _vmem_limit(block_bytes, resident_bytes):
    need = 2 * block_bytes + resident_bytes
    return int(min(need + need // 4, V7X_VMEM_BYTES - 8 * 1024 * 1024))


def _params(block_bytes, resident_bytes, ndim):
    return pltpu.CompilerParams(
        dimension_semantics=("arbitrary",) * ndim,
        vmem_limit_bytes=_vmem_limit(block_bytes, resident_bytes))


def _dot(a, b):
    return jnp.dot(a, b, preferred_element_type=_F32)


def _sub_tiles(rows, sub=None):
    sub = sub or SUB_TILE
    return [slice(r0, r0 + sub) for r0 in range(0, rows, sub)]


def _rms_unit(x):
    ms = jnp.mean(x * x, axis=-1, keepdims=True)
    return x * lax.rsqrt(ms + EPS)


def _gain_spec(layer):
    return pl.BlockSpec((1, 1, D_MODEL), lambda *_: (layer, 0, 0))


def _stage_rows(rows, cols):
    best = V7X_SUBLANES
    for cand in range(V7X_SUBLANES, rows + 1, V7X_SUBLANES):
        if rows % cand == 0 and cand * cols * 4 <= STAGE_CHUNK_BYTES:
            best = cand
    return best


def _stage_weights(first_step, jobs):
    @pl.when(first_step)
    def _():
        for src, dst in jobs:
            rows, cols = dst.shape
            chunk = _stage_rows(rows, cols)
            n_chunks = rows // chunk

            def load(stage, sem):
                def copy(c):
                    return pltpu.make_async_copy(
                        src.at[pl.ds(c * chunk, chunk)], stage.at[c % 2], sem.at[c % 2])

                copy(0).start()
                for c in range(n_chunks):
                    if c + 1 < n_chunks:
                        copy(c + 1).start()
                    copy(c).wait()
                    dst[c * chunk:(c + 1) * chunk, :] = stage[c % 2].astype(_BF16)

            pl.run_scoped(load, pltpu.VMEM((2, chunk, cols), _F32), pltpu.SemaphoreType.DMA((2,)))


def _memkv_kernel(mem_ref, g_ref, wkv_hbm, wq_hbm, wo_hbm, qk_ref, vo_ref, wkv_ref, wq_ref, wo_ref):
    layer = pl.program_id(0)
    _stage_weights(pl.program_id(1) == 0, [(wkv_hbm.at[layer], wkv_ref), (wq_hbm.at[layer], wq_ref),
                                           (wo_hbm.at[layer], wo_ref)])
    h = (_rms_unit(mem_ref[0]) * g_ref[0]).astype(_BF16)
    kv = _dot(h, wkv_ref[...]).astype(_BF16)
    for hd in range(MEM_HEADS):
        c0 = hd * MEM_HEAD_DIM
        m0 = hd * MEM_LEN
        k_h = kv[:, c0:c0 + MEM_HEAD_DIM]
        v_h = kv[:, D_MODEL + c0:D_MODEL + c0 + MEM_HEAD_DIM]
        qk = lax.dot_general(wq_ref[:, c0:c0 + MEM_HEAD_DIM], k_h, _NT, preferred_element_type=_F32)
        qk_ref[0, 0, :, m0:m0 + MEM_LEN] = (qk * (MEM_HEAD_DIM ** -0.5)).astype(_BF16)
        vo_ref[0, 0, m0:m0 + MEM_LEN, :] = _dot(v_h, wo_ref[c0:c0 + MEM_HEAD_DIM, :]).astype(_BF16)


def _memkv(mem, norm_memkv, w_kv, w_q, w_o):
    bsz = mem.shape[0]
    cols = MEM_HEADS * MEM_LEN
    blk = MEM_LEN * D_MODEL * 4 + 2 * D_MODEL * cols * 2
    resident = 4 * D_MODEL * D_MODEL * 2 + 2 * STAGE_CHUNK_BYTES + 6 * MEM_LEN * 2 * D_MODEL * 4
    w_scratch = pltpu.VMEM((D_MODEL, D_MODEL), _BF16)
    return pl.pallas_call(
        _memkv_kernel,
        grid=(DEPTH, bsz),
        in_specs=[
            pl.BlockSpec((1, MEM_LEN, D_MODEL), lambda l, b: (b, 0, 0)),
            pl.BlockSpec((1, 1, D_MODEL), lambda l, b: (l, 0, 0)),
            _HBM, _HBM, _HBM,
        ],
        out_specs=[
            pl.BlockSpec((1, 1, D_MODEL, cols), lambda l, b: (l, b, 0, 0)),
            pl.BlockSpec((1, 1, cols, D_MODEL), lambda l, b: (l, b, 0, 0)),
        ],
        out_shape=[
            jax.ShapeDtypeStruct((DEPTH, bsz, D_MODEL, cols), _BF16),
            jax.ShapeDtypeStruct((DEPTH, bsz, cols, D_MODEL), _BF16),
        ],
        scratch_shapes=[pltpu.VMEM((D_MODEL, 2 * D_MODEL), _BF16), w_scratch, w_scratch],
        compiler_params=_params(blk, resident, 2),
        name="memkv",
    )(mem, norm_memkv.reshape(DEPTH, 1, D_MODEL), w_kv, w_q, w_o)


def _shift_rows(z, prev, k):
    zk = pltpu.roll(z, k, 0)
    pk = pltpu.roll(prev, k, 0)
    r = lax.broadcasted_iota(jnp.int32, prev.shape, 0)
    top = jnp.where(r < k, pk, zk[:V7X_SUBLANES])
    return jnp.concatenate([top, zk[V7X_SUBLANES:]], axis=0)


def _convmix_kernel(x_ref, g_ref, win_hbm, cw_ref, wout_hbm, o_ref, win_ref, wout_ref, carry_ref,
                    *, tiles_per_seq):
    d = D_MODEL
    _stage_weights(pl.program_id(0) == 0, [(win_hbm.at[0], win_ref), (wout_hbm.at[0], wout_ref)])

    @pl.when(pl.program_id(0) % tiles_per_seq == 0)
    def _():
        carry_ref[...] = jnp.zeros_like(carry_ref)

    for rows in _sub_tiles(x_ref.shape[0]):
        x = x_ref[rows, :]
        h = (_rms_unit(x) * g_ref[0]).astype(_BF16)
        b_gate = _dot(h, win_ref[:, 0:d])
        z = _dot(h, win_ref[:, d:2 * d]) * _dot(h, win_ref[:, 2 * d:3 * d])
        prev = carry_ref[...]
        carry_ref[...] = z[z.shape[0] - V7X_SUBLANES:, :]
        conv = (cw_ref[0, 2:3, :] * z
                + cw_ref[0, 1:2, :] * _shift_rows(z, prev, 1)
                + cw_ref[0, 0:1, :] * _shift_rows(z, prev, 2))
        y = (b_gate * conv).astype(_BF16)
        o_ref[rows, :] = x + _dot(y, wout_ref[...])


def _convmix(x2d, norm_mix, w_in, w_conv, w_out, seq):
    n = x2d.shape[0]
    tm = TOKEN_TILE
    blk = 2 * tm * D_MODEL * 4
    resident = 4 * D_MODEL * D_MODEL * 2 + 2 * STAGE_CHUNK_BYTES + 10 * SUB_TILE * D_MODEL * 4
    return pl.pallas_call(
        functools.partial(_convmix_kernel, tiles_per_seq=seq // tm),
        grid=(n // tm,),
        in_specs=[
            pl.BlockSpec((tm, D_MODEL), lambda i: (i, 0)),
            _gain_spec(0),
            _HBM,
            pl.BlockSpec((1, CONV_W, D_MODEL), lambda i: (0, 0, 0)),
            _HBM,
        ],
        out_specs=pl.BlockSpec((tm, D_MODEL), lambda i: (i, 0)),
        out_shape=jax.ShapeDtypeStruct((n, D_MODEL), _F32),
        scratch_shapes=[
            pltpu.VMEM((D_MODEL, 3 * D_MODEL), _BF16),
            pltpu.VMEM((D_MODEL, D_MODEL), _BF16),
            pltpu.VMEM((V7X_SUBLANES, D_MODEL), _F32),
        ],
        compiler_params=_params(blk, resident, 1),
        name="convmix",
    )(x2d, norm_mix.reshape(DEPTH, 1, D_MODEL), w_in, w_conv, w_out)


def _memattn_kernel(*refs, has_pre):
    if has_pre:
        x_ref, a_ref, wpre_hbm, g_ref, qk_ref, vo_ref, o_ref, wpre_ref = refs
        _stage_weights(pl.program_id(0) == 0, [(wpre_hbm.at[0], wpre_ref)])
    else:
        x_ref, g_ref, qk_ref, vo_ref, o_ref = refs
    for rows in _sub_tiles(x_ref.shape[0]):
        x = x_ref[rows, :]
        if has_pre:
            x = x + _dot(a_ref[rows, :], wpre_ref[...])
        h = (_rms_unit(x) * g_ref[0]).astype(_BF16)
        scores = _dot(h, qk_ref[0, 0])
        probs = []
        for hd in range(MEM_HEADS):
            s = scores[:, hd * MEM_LEN:(hd + 1) * MEM_LEN]
            p = jnp.exp(s - jnp.max(s, axis=-1, keepdims=True))
            probs.append((p * (1.0 / jnp.sum(p, axis=-1, keepdims=True))).astype(_BF16))
        o_ref[rows, :] = x + _dot(jnp.concatenate(probs, axis=1), vo_ref[0, 0])


def _memattn(x2d, norm_mem, qk, vo, layer, seq, pre=None):
    n = x2d.shape[0]
    tm = TOKEN_TILE
    tiles_per_seq = seq // tm
    cols = MEM_HEADS * MEM_LEN
    row_spec = pl.BlockSpec((tm, D_MODEL), lambda i: (i, 0))
    in_specs = [row_spec]
    args = [x2d]
    scratch = []
    if pre is not None:
        in_specs += [row_spec, _HBM]
        args += list(pre)
        scratch.append(pltpu.VMEM((D_MODEL, D_MODEL), _BF16))
    in_specs += [
        _gain_spec(layer),
        pl.BlockSpec((1, 1, D_MODEL, cols), lambda i: (layer, i // tiles_per_seq, 0, 0)),
        pl.BlockSpec((1, 1, cols, D_MODEL), lambda i: (layer, i // tiles_per_seq, 0, 0)),
    ]
    args += [norm_mem.reshape(DEPTH, 1, D_MODEL), qk, vo]
    blk = 3 * tm * D_MODEL * 4 + 2 * D_MODEL * cols * 2
    resident = len(scratch) * D_MODEL * D_MODEL * 2 + 2 * STAGE_CHUNK_BYTES + 8 * SUB_TILE * D_MODEL * 4
    return pl.pallas_call(
        functools.partial(_memattn_kernel, has_pre=pre is not None),
        grid=(n // tm,),
        in_specs=in_specs,
        out_specs=row_spec,
        out_shape=jax.ShapeDtypeStruct((n, D_MODEL), _F32),
        scratch_shapes=scratch,
        compiler_params=_params(blk, resident, 1),
        name="memattn_l%d" % layer,
    )(*args)


def _ff_chunks():
    return [(c0, min(FF_CHUNK, D_FF - c0)) for c0 in range(0, D_FF, FF_CHUNK)]


def _swiglu_kernel(*refs, final, layer):
    if final:
        x_ref, g_ref, wgu_hbm, wd_hbm, gf_ref, o_ref, wgu_ref, wd_ref = refs
    else:
        x_ref, g_ref, wgu_hbm, wd_hbm, o_ref, wgu_ref, wd_ref = refs
    _stage_weights(pl.program_id(0) == 0, [(wgu_hbm.at[layer], wgu_ref), (wd_hbm.at[layer], wd_ref)])
    for rows in _sub_tiles(x_ref.shape[0], FFN_SUB_TILE):
        x = x_ref[rows, :]
        h = (_rms_unit(x) * g_ref[0]).astype(_BF16)
        acc = x
        for c0, cw in _ff_chunks():
            gate = _dot(h, wgu_ref[:, c0:c0 + cw])
            up = _dot(h, wgu_ref[:, D_FF + c0:D_FF + c0 + cw])
            act = (gate * (1.0 / (1.0 + jnp.exp(-gate))) * up).astype(_BF16)
            acc = acc + _dot(act, wd_ref[c0:c0 + cw, :])
        if final:
            acc = _rms_unit(acc) * gf_ref[0]
        o_ref[rows, :] = acc


def _swiglu(x2d, norm_ffn, w_gu, w_down, layer, final_gain=None):
    n = x2d.shape[0]
    tm = TOKEN_TILE
    in_specs = [pl.BlockSpec((tm, D_MODEL), lambda i: (i, 0)), _gain_spec(layer), _HBM, _HBM]
    args = [x2d, norm_ffn.reshape(DEPTH, 1, D_MODEL), w_gu, w_down]
    if final_gain is not None:
        in_specs.append(_gain_spec(0))
        args.append(final_gain.reshape(1, 1, D_MODEL))
    blk = 2 * tm * D_MODEL * 4
    resident = (3 * D_MODEL * D_FF * 2 + 2 * STAGE_CHUNK_BYTES
                + 4 * FFN_SUB_TILE * D_MODEL * 4 + 4 * FFN_SUB_TILE * FF_CHUNK * 4)
    return pl.pallas_call(
        functools.partial(_swiglu_kernel, final=final_gain is not None, layer=layer),
        grid=(n // tm,),
        in_specs=in_specs,
        out_specs=pl.BlockSpec((tm, D_MODEL), lambda i: (i, 0)),
        out_shape=jax.ShapeDtypeStruct((n, D_MODEL), _F32),
        scratch_shapes=[pltpu.VMEM((D_MODEL, 2 * D_FF), _BF16), pltpu.VMEM((D_FF, D_MODEL), _BF16)],
        compiler_params=_params(blk, resident, 1),
        name="swiglu_final" if final_gain is not None else "swiglu",
    )(*args)


def _rope(x, cos, sin_signed, first_half):
    d = x.shape[1]
    rot = jnp.where(first_half, pltpu.roll(x, d - ROT_DIM // 2, 1), pltpu.roll(x, ROT_DIM // 2, 1))
    return x * cos + rot * sin_signed


def _qkv_kernel(x_ref, pos_ref, invf_ref, gq_ref, gk_ref, wq_hbm, wkv_hbm,
                q_ref, k_ref, vt_ref, km_ref, wq_ref, wkv_ref, *, tiles_per_seq):
    d = D_MODEL
    tm = x_ref.shape[0]
    blocks_per_tile = tm // BLOCK
    s_tile = pl.program_id(0) % tiles_per_seq
    _stage_weights(pl.program_id(0) == 0, [(wq_hbm.at[0], wq_ref), (wkv_hbm, wkv_ref)])

    @pl.when(s_tile == 0)
    def _():
        km_ref[...] = jnp.zeros_like(km_ref)

    lane = lax.broadcasted_iota(jnp.int32, (1, HEAD_DIM), 1)
    lane_d = lax.broadcasted_iota(jnp.int32, (1, d), 1)
    first_half = (lane_d % HEAD_DIM) < ROT_DIM // 2
    row16 = lax.broadcasted_iota(jnp.int32, km_ref.shape[1:], 0)

    for rows in _sub_tiles(tm):
        xh = _rms_unit(x_ref[rows, :])
        hq = (xh * gq_ref[0]).astype(_BF16)
        hk = (xh * gk_ref[0]).astype(_BF16)

        pos = pos_ref[0, :, rows].astype(_F32)
        pos_rows = jnp.broadcast_to(pos, (V7X_LANES, SUB_TILE)).T
        ang = pos_rows * invf_ref[...]
        cos = jnp.cos(ang)
        sin_signed = jnp.where(lane < ROT_DIM // 2, -1.0, 1.0) * jnp.sin(ang)
        cos = jnp.concatenate([cos] * N_HEADS, axis=1)
        sin_signed = jnp.concatenate([sin_signed] * N_HEADS, axis=1)

        q = _dot(hq, wq_ref[...]) * (HEAD_DIM ** -0.5 * math.log2(math.e))
        q_ref[rows, :] = _rope(q, cos, sin_signed, first_half).astype(_BF16)

        k = _rope(_dot(hk, wkv_ref[:, 0:d]), cos, sin_signed, first_half)
        k_ref[rows, :] = k.astype(_BF16)
        v = _dot(hk, wkv_ref[:, d:2 * d])

        for r0 in range(0, SUB_TILE, BLOCK):
            j = (rows.start + r0) // BLOCK
            vt_ref[0, j] = v[r0:r0 + BLOCK, :].T.astype(_BF16)
            mean_row = jnp.sum(k[r0:r0 + BLOCK, :], axis=0, keepdims=True) * (1.0 / BLOCK)
            km_ref[0] = jnp.where(row16 == s_tile * blocks_per_tile + j, mean_row, km_ref[0])


def _qkv(x2d, positions, norm_mix, kv_norm, w_q, w_kv, bsz, seq):
    n = x2d.shape[0]
    tm = QKV_TOKEN_TILE
    tiles_per_seq = seq // tm
    n_blocks = seq // BLOCK
    inv_freq = ROPE_THETA ** (-jnp.arange(0, ROT_DIM, 2, dtype=_F32) / ROT_DIM)
    invf_lane = jnp.concatenate(
        [inv_freq, inv_freq, jnp.zeros((HEAD_DIM - ROT_DIM,), _F32)]).reshape(1, HEAD_DIM)
    row_spec = pl.BlockSpec((tm, D_MODEL), lambda i: (i, 0))
    blk = tm * D_MODEL * 4 + 3 * tm * D_MODEL * 2 + n_blocks * D_MODEL * 4
    resident = 3 * D_MODEL * D_MODEL * 2 + 2 * STAGE_CHUNK_BYTES + 12 * SUB_TILE * D_MODEL * 4
    return pl.pallas_call(
        functools.partial(_qkv_kernel, tiles_per_seq=tiles_per_seq),
        grid=(n // tm,),
        in_specs=[
            row_spec,
            pl.BlockSpec((1, 1, tm), lambda i: (i, 0, 0)),
            pl.BlockSpec((1, HEAD_DIM), lambda i: (0, 0)),
            _gain_spec(1),
            _gain_spec(0),
            _HBM,
            _HBM,
        ],
        out_specs=[
            row_spec,
            row_spec,
            pl.BlockSpec((1, tm // BLOCK, D_MODEL, BLOCK),
                         lambda i: (i // tiles_per_seq, i % tiles_per_seq, 0, 0)),
            pl.BlockSpec((1, n_blocks, D_MODEL), lambda i: (i // tiles_per_seq, 0, 0)),
        ],
        out_shape=[
            jax.ShapeDtypeStruct((n, D_MODEL), _BF16),
            jax.ShapeDtypeStruct((n, D_MODEL), _BF16),
            jax.ShapeDtypeStruct((bsz, n_blocks, D_MODEL, BLOCK), _BF16),
            jax.ShapeDtypeStruct((bsz, n_blocks, D_MODEL), _F32),
        ],
        scratch_shapes=[pltpu.VMEM((D_MODEL, D_MODEL), _BF16), pltpu.VMEM((D_MODEL, 2 * D_MODEL), _BF16)],
        compiler_params=_params(blk, resident, 1),
        name="qkv",
    )(x2d, positions.reshape(n // tm, 1, tm), invf_lane,
      norm_mix.reshape(DEPTH, 1, D_MODEL), kv_norm.reshape(1, 1, D_MODEL), w_q, w_kv)


def _moba_kernel(q_ref, k_ref, vt_ref, km_ref, o_ref, bias_ref, m_ref, l_ref, acc_ref):
    t = pl.program_id(1)
    n_blocks = km_ref.shape[2]
    own_rows = pl.ds(pl.multiple_of(t * BLOCK, BLOCK), BLOCK)
    chains = [(a, slice(h * HEAD_DIM, (h + 1) * HEAD_DIM))
              for a in range(q_ref.shape[0]) for h in range(N_HEADS)]

    def score_t(i, rows):
        a, c = chains[i]
        return lax.dot_general(k_ref[a, 0, rows, c], q_ref[a, 0, :, c], _NT, preferred_element_type=_F32)

    gates = []
    for a, c in chains:
        km = km_ref[a, 0, :, c]
        km_hi = km.astype(_BF16)
        km_lo = (km - km_hi.astype(_F32)).astype(_BF16)
        gates.append(lax.dot_general(km_hi, q_ref[a, 0, :, c], _NT, preferred_element_type=_F32)
                     + lax.dot_general(km_lo, q_ref[a, 0, :, c], _NT, preferred_element_type=_F32))
    own_scores = [score_t(i, own_rows) for i in range(len(chains))]

    blk = lax.broadcasted_iota(jnp.int32, (n_blocks, BLOCK), 0)
    for i, gate in enumerate(gates):
        rank = jnp.zeros(gate.shape, _F32)
        for m in range(n_blocks):
            g_m = gate[m:m + 1, :]
            beats = jnp.where(g_m > gate, 1.0, jnp.where(g_m == gate, jnp.where(blk > m, 1.0, 0.0), 0.0))
            rank = rank + jnp.where(m < t, beats, 0.0)
        selected = jnp.where(blk < t, rank, float(TOPK)) < TOPK
        bias_ref[i] = jnp.where(selected, 0.0, -jnp.inf)

    ones_rows = jnp.ones((_BF16_SUBLANES, BLOCK), _BF16)

    kpos = lax.broadcasted_iota(jnp.int32, (BLOCK, BLOCK), 0)
    qpos = lax.broadcasted_iota(jnp.int32, (BLOCK, BLOCK), 1)
    for i, (a, c) in enumerate(chains):
        s = jnp.where(kpos <= qpos, own_scores[i], -jnp.inf)
        m0 = jnp.max(s, axis=0, keepdims=True)
        p = jnp.exp2(s - m0).astype(_BF16)
        pv = _dot(jnp.concatenate([vt_ref[a, 0, t, c, :], ones_rows], axis=0), p)
        m_ref[i:i + 1, :] = m0
        l_ref[i:i + 1, :] = pv[HEAD_DIM:HEAD_DIM + 1]
        acc_ref[i] = pv[:HEAD_DIM]

    def past_block_pair(j, carry):
        n0 = 2 * j
        rows = pl.ds(pl.multiple_of(n0 * BLOCK, 2 * BLOCK), 2 * BLOCK)
        scores = {i: score_t(i, rows) for i in range(MOBA_SCORE_LOOKAHEAD)}
        for i, (a, c) in enumerate(chains):
            s = scores.pop(i)
            s_a, s_b = s[:BLOCK], s[BLOCK:]
            bias_a = bias_ref[i, pl.ds(n0, 1), :]
            bias_b = bias_ref[i, pl.ds(n0 + 1, 1), :]
            m_run = m_ref[i:i + 1, :]
            m_new = jnp.maximum(m_run, jnp.maximum(jnp.max(s_a, axis=0, keepdims=True) + bias_a,
                                                   jnp.max(s_b, axis=0, keepdims=True) + bias_b))
            m_ref[i:i + 1, :] = m_new
            alpha = jnp.exp2(m_run - m_new)
            p_a = jnp.exp2(s_a - (m_new - bias_a)).astype(_BF16)
            p_b = jnp.exp2(s_b - (m_new - bias_b)).astype(_BF16)
            vt_a = jnp.concatenate([vt_ref[a, 0, n0, c, :], ones_rows], axis=0)
            vt_b = jnp.concatenate([vt_ref[a, 0, n0 + 1, c, :], ones_rows], axis=0)
            pv = _dot(vt_a, p_a) + _dot(vt_b, p_b)
            if i + MOBA_SCORE_LOOKAHEAD < len(chains):
                scores[i + MOBA_SCORE_LOOKAHEAD] = score_t(i + MOBA_SCORE_LOOKAHEAD, rows)
            acc_ref[i] = alpha * acc_ref[i] + pv[:HEAD_DIM]
            l_ref[i:i + 1, :] = alpha * l_ref[i:i + 1, :] + pv[HEAD_DIM:HEAD_DIM + 1]
        return carry

    lax.fori_loop(0, lax.shift_right_logical(t + 1, 1), past_block_pair, 0)
    for i, (a, c) in enumerate(chains):
        out = acc_ref[i] * (1.0 / l_ref[i:i + 1, :])
        o_ref[a, 0, :, c] = out.T.astype(_BF16)


def _moba(q, k, vt, k_mean, bsz, seq):
    n_blocks = seq // BLOCK
    bps = MOBA_BATCHES_PER_STEP
    groups = bsz // bps
    n_chains = bps * N_HEADS
    q4 = q.reshape(bps, groups, seq, D_MODEL)
    k4 = k.reshape(bps, groups, seq, D_MODEL)
    vt5 = vt.reshape(bps, groups, n_blocks, D_MODEL, BLOCK)
    km4 = k_mean.reshape(bps, groups, n_blocks, D_MODEL)
    single = pl.Buffered(1)
    blk = bps * (2 * BLOCK * D_MODEL * 2 + seq * D_MODEL * 2 + n_blocks * D_MODEL * 4)
    scratch = n_chains * (n_blocks + HEAD_DIM + 2) * BLOCK * 4
    out = pl.pallas_call(
        _moba_kernel,
        grid=(groups, n_blocks),
        in_specs=[
            pl.BlockSpec((bps, 1, BLOCK, D_MODEL), lambda g, t: (0, g, t, 0)),
            pl.BlockSpec((bps, 1, seq, D_MODEL), lambda g, t: (0, g, 0, 0), pipeline_mode=single),
            pl.BlockSpec((bps, 1, n_blocks, D_MODEL, BLOCK), lambda g, t: (0, g, 0, 0, 0),
                         pipeline_mode=single),
            pl.BlockSpec((bps, 1, n_blocks, D_MODEL), lambda g, t: (0, g, 0, 0)),
        ],
        out_specs=pl.BlockSpec((bps, 1, BLOCK, D_MODEL), lambda g, t: (0, g, t, 0)),
        out_shape=jax.ShapeDtypeStruct((bps, groups, seq, D_MODEL), _BF16),
        scratch_shapes=[
            pltpu.VMEM((n_chains, n_blocks, BLOCK), _F32),
            pltpu.VMEM((n_chains, BLOCK), _F32),
            pltpu.VMEM((n_chains, BLOCK), _F32),
            pltpu.VMEM((n_chains, HEAD_DIM, BLOCK), _F32),
        ],
        compiler_params=_params(blk, scratch + n_chains * 6 * BLOCK * BLOCK * 4, 2),
        name="moba",
    )(q4, k4, vt5, km4)
    return out.reshape(bsz * seq, D_MODEL)


def kernel(x, mem, positions, norm_mix, norm_mem, norm_memkv, norm_ffn, norm_final,
           conv_w_in, conv_w, conv_w_out, kv_norm, w_kv, moba_w_q, moba_w_o,
           mem_w_q, mem_w_kv, mem_w_o, ffn_w_gu, ffn_w_down):
    bsz, seq, d = x.shape
    assert d == D_MODEL and seq % TOKEN_TILE == 0 and seq % QKV_TOKEN_TILE == 0
    assert TOKEN_TILE % SUB_TILE == 0 and QKV_TOKEN_TILE % SUB_TILE == 0 and SUB_TILE % BLOCK == 0
    assert TOKEN_TILE % FFN_SUB_TILE == 0
    assert mem.shape == (bsz, MEM_LEN, D_MODEL)
    assert bsz % MOBA_BATCHES_PER_STEP == 0
    assert norm_mix.shape[0] == DEPTH == 2 and conv_w_in.shape[0] == 1 and moba_w_q.shape[0] == 1

    x2d = x.reshape(bsz * seq, d)
    mem_qk, mem_vo = _memkv(mem, norm_memkv, mem_w_kv, mem_w_q, mem_w_o)

    x2d = _convmix(x2d, norm_mix, conv_w_in, conv_w, conv_w_out, seq)
    x2d = _memattn(x2d, norm_mem, mem_qk, mem_vo, 0, seq)
    x2d = _swiglu(x2d, norm_ffn, ffn_w_gu, ffn_w_down, 0)

    q, k, vt, k_mean = _qkv(x2d, positions, norm_mix, kv_norm, moba_w_q, w_kv, bsz, seq)
    attn = _moba(q, k, vt, k_mean, bsz, seq)

    x2d = _memattn(x2d, norm_mem, mem_qk, mem_vo, 1, seq, pre=(attn, moba_w_o))
    x2d = _swiglu(x2d, norm_ffn, ffn_w_gu, ffn_w_down, 1, final_gain=norm_final)
    return x2d.reshape(bsz, seq, d)
```

```python
import functools
import math

import jax
import jax.numpy as jnp
from jax import lax
from jax.experimental import pallas as pl
from jax.experimental.pallas import tpu as pltpu

D_MODEL = 1024
DEPTH = 2
CONV_W = 3
N_HEADS = 8
HEAD_DIM = D_MODEL // N_HEADS
BLOCK = 256
TOPK = 3
ROT_DIM = HEAD_DIM // 4
ROPE_THETA = 500000.0
MEM_LEN = 256
MEM_HEADS = 4
MEM_HEAD_DIM = D_MODEL // MEM_HEADS
D_FF = ((8 * D_MODEL + 3 * 256 - 1) // (3 * 256)) * 256
EPS = 1e-6

V7X_VMEM_BYTES = 64 * 1024 * 1024
V7X_LANES = 128
V7X_SUBLANES = 8
V7X_MXU_DIM = 256
_BF16_SUBLANES = 2 * V7X_SUBLANES

TOKEN_TILE = 1024
QKV_TOKEN_TILE = 512
SUB_TILE = 512
FFN_SUB_TILE = 256
FF_CHUNK = 4 * V7X_MXU_DIM
MOBA_BATCHES_PER_STEP = 2
MOBA_SCORE_LOOKAHEAD = 4
STAGE_CHUNK_BYTES = 3 * 1024 * 1024

_F32 = jnp.float32
_BF16 = jnp.bfloat16
_NT = (((1,), (1,)), ((), ()))
_HBM = pl.BlockSpec(memory_space=pl.ANY)


def _vmem_limit(block_bytes, resident_bytes):
    need = 2 * block_bytes + resident_bytes
    return int(min(need + need // 4, V7X_VMEM_BYTES - 8 * 1024 * 1024))


def _params(block_bytes, resident_bytes, ndim):
    return pltpu.CompilerParams(
        dimension_semantics=("arbitrary",) * ndim,
        vmem_limit_bytes=_vmem_limit(block_bytes, resident_bytes))


def _dot(a, b):
    return jnp.dot(a, b, preferred_element_type=_F32)


def _sub_tiles(rows, sub=None):
    sub = sub or SUB_TILE
    return [slice(r0, r0 + sub) for r0 in range(0, rows, sub)]


def _rms_unit(x):
    ms = jnp.mean(x * x, axis=-1, keepdims=True)
    return x * lax.rsqrt(ms + EPS)


def _gain_spec(layer):
    return pl.BlockSpec((1, 1, D_MODEL), lambda *_: (layer, 0, 0))


def _stage_rows(rows, cols):
    best = V7X_SUBLANES
    for cand in range(V7X_SUBLANES, rows + 1, V7X_SUBLANES):
        if rows % cand == 0 and cand * cols * 4 <= STAGE_CHUNK_BYTES:
            best = cand
    return best


def _stage_weights(first_step, jobs):
    @pl.when(first_step)
    def _():
        for src, dst in jobs:
            rows, cols = dst.shape
            chunk = _stage_rows(rows, cols)
            n_chunks = rows // chunk

            def load(stage, sem):
                def copy(c):
                    return pltpu.make_async_copy(
                        src.at[pl.ds(c * chunk, chunk)], stage.at[c % 2], sem.at[c % 2])

                copy(0).start()
                for c in range(n_chunks):
                    if c + 1 < n_chunks:
                        copy(c + 1).start()
                    copy(c).wait()
                    dst[c * chunk:(c + 1) * chunk, :] = stage[c % 2].astype(_BF16)

            pl.run_scoped(load, pltpu.VMEM((2, chunk, cols), _F32), pltpu.SemaphoreType.DMA((2,)))


def _memkv_kernel(mem_ref, g_ref, wkv_hbm, wq_hbm, wo_hbm, qk_ref, vo_ref, wkv_ref, wq_ref, wo_ref):
    layer = pl.program_id(0)
    _stage_weights(True, [(wkv_hbm.at[layer], wkv_ref), (wq_hbm.at[layer], wq_ref), (wo_hbm.at[layer], wo_ref)])
    bsz = mem_ref.shape[0]
    mem = mem_ref[...].reshape(bsz * MEM_LEN, D_MODEL)
    h = (_rms_unit(mem) * g_ref[0]).astype(_BF16)
    kv = _dot(h, wkv_ref[...]).astype(_BF16)
    for hd in range(MEM_HEADS):
        c0 = hd * MEM_HEAD_DIM
        m0 = hd * MEM_LEN
        for b in range(bsz):
            k_h = kv[b * MEM_LEN:(b + 1) * MEM_LEN, c0:c0 + MEM_HEAD_DIM]
            qk = lax.dot_general(wq_ref[:, c0:c0 + MEM_HEAD_DIM], k_h, _NT, preferred_element_type=_F32)
            qk_ref[0, b, :, m0:m0 + MEM_LEN] = (qk * (MEM_HEAD_DIM ** -0.5)).astype(_BF16)
        vo = _dot(kv[:, D_MODEL + c0:D_MODEL + c0 + MEM_HEAD_DIM], wo_ref[c0:c0 + MEM_HEAD_DIM, :])
        for b in range(bsz):
            vo_ref[0, b, m0:m0 + MEM_LEN, :] = vo[b * MEM_LEN:(b + 1) * MEM_LEN, :].astype(_BF16)


def _memkv(mem, norm_memkv, w_kv, w_q, w_o):
    bsz = mem.shape[0]
    cols = MEM_HEADS * MEM_LEN
    single = pl.Buffered(1)
    blk = bsz * MEM_LEN * D_MODEL * 4
    resident = (2 * bsz * D_MODEL * cols * 2 + 4 * D_MODEL * D_MODEL * 2 + 2 * STAGE_CHUNK_BYTES
                + 6 * bsz * MEM_LEN * 2 * D_MODEL * 2)
    w_scratch = pltpu.VMEM((D_MODEL, D_MODEL), _BF16)
    return pl.pallas_call(
        _memkv_kernel,
        grid=(DEPTH,),
        in_specs=[
            pl.BlockSpec((bsz, MEM_LEN, D_MODEL), lambda l: (0, 0, 0)),
            pl.BlockSpec((1, 1, D_MODEL), lambda l: (l, 0, 0)),
            _HBM, _HBM, _HBM,
        ],
        out_specs=[
            pl.BlockSpec((1, bsz, D_MODEL, cols), lambda l: (l, 0, 0, 0), pipeline_mode=single),
            pl.BlockSpec((1, bsz, cols, D_MODEL), lambda l: (l, 0, 0, 0), pipeline_mode=single),
        ],
        out_shape=[
            jax.ShapeDtypeStruct((DEPTH, bsz, D_MODEL, cols), _BF16),
            jax.ShapeDtypeStruct((DEPTH, bsz, cols, D_MODEL), _BF16),
        ],
        scratch_shapes=[pltpu.VMEM((D_MODEL, 2 * D_MODEL), _BF16), w_scratch, w_scratch],
        compiler_params=_params(blk, resident, 1),
        name="memkv",
    )(mem, norm_memkv.reshape(DEPTH, 1, D_MODEL), w_kv, w_q, w_o)


def _shift_rows(z, prev, k):
    zk = pltpu.roll(z, k, 0)
    pk = pltpu.roll(prev, k, 0)
    r = lax.broadcasted_iota(jnp.int32, prev.shape, 0)
    top = jnp.where(r < k, pk, zk[:V7X_SUBLANES])
    return jnp.concatenate([top, zk[V7X_SUBLANES:]], axis=0)


def _convmix_kernel(x_ref, g_ref, win_hbm, cw_ref, wout_hbm, o_ref, win_ref, wout_ref, carry_ref,
                    *, tiles_per_seq):
    d = D_MODEL
    _stage_weights(pl.program_id(0) == 0, [(win_hbm.at[0], win_ref), (wout_hbm.at[0], wout_ref)])

    @pl.when(pl.program_id(0) % tiles_per_seq == 0)
    def _():
        carry_ref[...] = jnp.zeros_like(carry_ref)

    for rows in _sub_tiles(x_ref.shape[0]):
        x = x_ref[rows, :]
        h = (_rms_unit(x) * g_ref[0]).astype(_BF16)
        b_gate = _dot(h, win_ref[:, 0:d])
        z = _dot(h, win_ref[:, d:2 * d]) * _dot(h, win_ref[:, 2 * d:3 * d])
        prev = carry_ref[...]
        carry_ref[...] = z[z.shape[0] - V7X_SUBLANES:, :]
        conv = (cw_ref[0, 2:3, :] * z
                + cw_ref[0, 1:2, :] * _shift_rows(z, prev, 1)
                + cw_ref[0, 0:1, :] * _shift_rows(z, prev, 2))
        y = (b_gate * conv).astype(_BF16)
        o_ref[rows, :] = x + _dot(y, wout_ref[...])


def _convmix(x2d, norm_mix, w_in, w_conv, w_out, seq):
    n = x2d.shape[0]
    tm = TOKEN_TILE
    blk = 2 * tm * D_MODEL * 4
    resident = 4 * D_MODEL * D_MODEL * 2 + 2 * STAGE_CHUNK_BYTES + 10 * SUB_TILE * D_MODEL * 4
    return pl.pallas_call(
        functools.partial(_convmix_kernel, tiles_per_seq=seq // tm),
        grid=(n // tm,),
        in_specs=[
            pl.BlockSpec((tm, D_MODEL), lambda i: (i, 0)),
            _gain_spec(0),
            _HBM,
            pl.BlockSpec((1, CONV_W, D_MODEL), lambda i: (0, 0, 0)),
            _HBM,
        ],
        out_specs=pl.BlockSpec((tm, D_MODEL), lambda i: (i, 0)),
        out_shape=jax.ShapeDtypeStruct((n, D_MODEL), _F32),
        scratch_shapes=[
            pltpu.VMEM((D_MODEL, 3 * D_MODEL), _BF16),
            pltpu.VMEM((D_MODEL, D_MODEL), _BF16),
            pltpu.VMEM((V7X_SUBLANES, D_MODEL), _F32),
        ],
        compiler_params=_params(blk, resident, 1),
        name="convmix",
    )(x2d, norm_mix.reshape(DEPTH, 1, D_MODEL), w_in, w_conv, w_out)


def _memattn_kernel(*refs, has_pre):
    if has_pre:
        x_ref, a_ref, wpre_hbm, g_ref, qk_ref, vo_ref, o_ref, wpre_ref = refs
        _stage_weights(pl.program_id(0) == 0, [(wpre_hbm.at[0], wpre_ref)])
    else:
        x_ref, g_ref, qk_ref, vo_ref, o_ref = refs
    for rows in _sub_tiles(x_ref.shape[0]):
        x = x_ref[rows, :]
        if has_pre:
            x = x + _dot(a_ref[rows, :], wpre_ref[...])
        h = (_rms_unit(x) * g_ref[0]).astype(_BF16)
        scores = _dot(h, qk_ref[0, 0])
        probs = []
        for hd in range(MEM_HEADS):
            s = scores[:, hd * MEM_LEN:(hd + 1) * MEM_LEN]
            p = jnp.exp(s - jnp.max(s, axis=-1, keepdims=True))
            probs.append((p * (1.0 / jnp.sum(p, axis=-1, keepdims=True))).astype(_BF16))
        o_ref[rows, :] = x + _dot(jnp.concatenate(probs, axis=1), vo_ref[0, 0])


def _memattn(x2d, norm_mem, qk, vo, layer, seq, pre=None):
    n = x2d.shape[0]
    tm = TOKEN_TILE
    tiles_per_seq = seq // tm
    cols = MEM_HEADS * MEM_LEN
    row_spec = pl.BlockSpec((tm, D_MODEL), lambda i: (i, 0))
    in_specs = [row_spec]
    args = [x2d]
    scratch = []
    if pre is not None:
        in_specs += [row_spec, _HBM]
        args += list(pre)
        scratch.append(pltpu.VMEM((D_MODEL, D_MODEL), _BF16))
    in_specs += [
        _gain_spec(layer),
        pl.BlockSpec((1, 1, D_MODEL, cols), lambda i: (layer, i // tiles_per_seq, 0, 0)),
        pl.BlockSpec((1, 1, cols, D_MODEL), lambda i: (layer, i // tiles_per_seq, 0, 0)),
    ]
    args += [norm_mem.reshape(DEPTH, 1, D_MODEL), qk, vo]
    blk = 3 * tm * D_MODEL * 4 + 2 * D_MODEL * cols * 2
    resident = len(scratch) * D_MODEL * D_MODEL * 2 + 2 * STAGE_CHUNK_BYTES + 8 * SUB_TILE * D_MODEL * 4
    return pl.pallas_call(
        functools.partial(_memattn_kernel, has_pre=pre is not None),
        grid=(n // tm,),
        in_specs=in_specs,
        out_specs=row_spec,
        out_shape=jax.ShapeDtypeStruct((n, D_MODEL), _F32),
        scratch_shapes=scratch,
        compiler_params=_params(blk, resident, 1),
        name="memattn_l%d" % layer,
    )(*args)


def _ff_chunks():
    return [(c0, min(FF_CHUNK, D_FF - c0)) for c0 in range(0, D_FF, FF_CHUNK)]


def _swiglu_kernel(*refs, final, layer):
    if final:
        x_ref, g_ref, wgu_hbm, wd_hbm, gf_ref, o_ref, wgu_ref, wd_ref = refs
    else:
        x_ref, g_ref, wgu_hbm, wd_hbm, o_ref, wgu_ref, wd_ref = refs
    _stage_weights(pl.program_id(0) == 0, [(wgu_hbm.at[layer], wgu_ref), (wd_hbm.at[layer], wd_ref)])
    for rows in _sub_tiles(x_ref.shape[0], FFN_SUB_TILE):
        x = x_ref[rows, :]
        h = (_rms_unit(x) * g_ref[0]).astype(_BF16)
        acc = x
        for c0, cw in _ff_chunks():
            gate = _dot(h, wgu_ref[:, c0:c0 + cw])
            up = _dot(h, wgu_ref[:, D_FF + c0:D_FF + c0 + cw])
            act = (gate * (1.0 / (1.0 + jnp.exp(-gate))) * up).astype(_BF16)
            acc = acc + _dot(act, wd_ref[c0:c0 + cw, :])
        if final:
            acc = _rms_unit(acc) * gf_ref[0]
        o_ref[rows, :] = acc


def _swiglu(x2d, norm_ffn, w_gu, w_down, layer, final_gain=None):
    n = x2d.shape[0]
    tm = TOKEN_TILE
    in_specs = [pl.BlockSpec((tm, D_MODEL), lambda i: (i, 0)), _gain_spec(layer), _HBM, _HBM]
    args = [x2d, norm_ffn.reshape(DEPTH, 1, D_MODEL), w_gu, w_down]
    if final_gain is not None:
        in_specs.append(_gain_spec(0))
        args.append(final_gain.reshape(1, 1, D_MODEL))
    blk = 2 * tm * D_MODEL * 4
    resident = (3 * D_MODEL * D_FF * 2 + 2 * STAGE_CHUNK_BYTES
                + 4 * FFN_SUB_TILE * D_MODEL * 4 + 4 * FFN_SUB_TILE * FF_CHUNK * 4)
    return pl.pallas_call(
        functools.partial(_swiglu_kernel, final=final_gain is not None, layer=layer),
        grid=(n // tm,),
        in_specs=in_specs,
        out_specs=pl.BlockSpec((tm, D_MODEL), lambda i: (i, 0)),
        out_shape=jax.ShapeDtypeStruct((n, D_MODEL), _F32),
        scratch_shapes=[pltpu.VMEM((D_MODEL, 2 * D_FF), _BF16), pltpu.VMEM((D_FF, D_MODEL), _BF16)],
        compiler_params=_params(blk, resident, 1),
        name="swiglu_final" if final_gain is not None else "swiglu",
    )(*args)


def _rope(x, cos, sin_signed, first_half):
    d = x.shape[1]
    rot = jnp.where(first_half, pltpu.roll(x, d - ROT_DIM // 2, 1), pltpu.roll(x, ROT_DIM // 2, 1))
    return x * cos + rot * sin_signed


def _qkv_kernel(x_ref, pos_ref, invf_ref, gq_ref, gk_ref, wq_hbm, wkv_hbm,
                q_ref, k_ref, vt_ref, km_ref, wq_ref, wkv_ref, *, tiles_per_seq):
    d = D_MODEL
    tm = x_ref.shape[0]
    blocks_per_tile = tm // BLOCK
    s_tile = pl.program_id(0) % tiles_per_seq
    _stage_weights(pl.program_id(0) == 0, [(wq_hbm.at[0], wq_ref), (wkv_hbm, wkv_ref)])

    @pl.when(s_tile == 0)
    def _():
        km_ref[...] = jnp.zeros_like(km_ref)

    lane = lax.broadcasted_iota(jnp.int32, (1, HEAD_DIM), 1)
    lane_d = lax.broadcasted_iota(jnp.int32, (1, d), 1)
    first_half = (lane_d % HEAD_DIM) < ROT_DIM // 2
    row16 = lax.broadcasted_iota(jnp.int32, km_ref.shape[1:], 0)

    for rows in _sub_tiles(tm):
        xh = _rms_unit(x_ref[rows, :])
        hq = (xh * gq_ref[0]).astype(_BF16)
        hk = (xh * gk_ref[0]).astype(_BF16)

        pos = pos_ref[0, :, rows].astype(_F32)
        pos_rows = jnp.broadcast_to(pos, (V7X_LANES, SUB_TILE)).T
        ang = pos_rows * invf_ref[...]
        cos = jnp.cos(ang)
        sin_signed = jnp.where(lane < ROT_DIM // 2, -1.0, 1.0) * jnp.sin(ang)
        cos = jnp.concatenate([cos] * N_HEADS, axis=1)
        sin_signed = jnp.concatenate([sin_signed] * N_HEADS, axis=1)

        q = _dot(hq, wq_ref[...]) * (HEAD_DIM ** -0.5 * math.log2(math.e))
        q_ref[rows, :] = _rope(q, cos, sin_signed, first_half).astype(_BF16)

        k = _rope(_dot(hk, wkv_ref[:, 0:d]), cos, sin_signed, first_half)
        k_ref[rows, :] = k.astype(_BF16)
        v = _dot(hk, wkv_ref[:, d:2 * d])

        for r0 in range(0, SUB_TILE, BLOCK):
            j = (rows.start + r0) // BLOCK
            vt_ref[0, j] = v[r0:r0 + BLOCK, :].T.astype(_BF16)
            mean_row = jnp.sum(k[r0:r0 + BLOCK, :], axis=0, keepdims=True) * (1.0 / BLOCK)
            km_ref[0] = jnp.where(row16 == s_tile * blocks_per_tile + j, mean_row, km_ref[0])


def _qkv(x2d, positions, norm_mix, kv_norm, w_q, w_kv, bsz, seq):
    n = x2d.shape[0]
    tm = QKV_TOKEN_TILE
    tiles_per_seq = seq // tm
    n_blocks = seq // BLOCK
    inv_freq = ROPE_THETA ** (-jnp.arange(0, ROT_DIM, 2, dtype=_F32) / ROT_DIM)
    invf_lane = jnp.concatenate(
        [inv_freq, inv_freq, jnp.zeros((HEAD_DIM - ROT_DIM,), _F32)]).reshape(1, HEAD_DIM)
    row_spec = pl.BlockSpec((tm, D_MODEL), lambda i: (i, 0))
    blk = tm * D_MODEL * 4 + 3 * tm * D_MODEL * 2 + n_blocks * D_MODEL * 4
    resident = 3 * D_MODEL * D_MODEL * 2 + 2 * STAGE_CHUNK_BYTES + 12 * SUB_TILE * D_MODEL * 4
    return pl.pallas_call(
        functools.partial(_qkv_kernel, tiles_per_seq=tiles_per_seq),
        grid=(n // tm,),
        in_specs=[
            row_spec,
            pl.BlockSpec((1, 1, tm), lambda i: (i, 0, 0)),
            pl.BlockSpec((1, HEAD_DIM), lambda i: (0, 0)),
            _gain_spec(1),
            _gain_spec(0),
            _HBM,
            _HBM,
        ],
        out_specs=[
            row_spec,
            row_spec,
            pl.BlockSpec((1, tm // BLOCK, D_MODEL, BLOCK),
                         lambda i: (i // tiles_per_seq, i % tiles_per_seq, 0, 0)),
            pl.BlockSpec((1, n_blocks, D_MODEL), lambda i: (i // tiles_per_seq, 0, 0)),
        ],
        out_shape=[
            jax.ShapeDtypeStruct((n, D_MODEL), _BF16),
            jax.ShapeDtypeStruct((n, D_MODEL), _BF16),
            jax.ShapeDtypeStruct((bsz, n_blocks, D_MODEL, BLOCK), _BF16),
            jax.ShapeDtypeStruct((bsz, n_blocks, D_MODEL), _F32),
        ],
        scratch_shapes=[pltpu.VMEM((D_MODEL, D_MODEL), _BF16), pltpu.VMEM((D_MODEL, 2 * D_MODEL), _BF16)],
        compiler_params=_params(blk, resident, 1),
        name="qkv",
    )(x2d, positions.reshape(n // tm, 1, tm), invf_lane,
      norm_mix.reshape(DEPTH, 1, D_MODEL), kv_norm.reshape(1, 1, D_MODEL), w_q, w_kv)


def _moba_kernel(q_ref, k_hbm, vt_hbm, km_ref, o_ref, k_ref, vt_ref, kv_sem, bias_ref, m_ref, l_ref, acc_ref):
    g = pl.program_id(0)
    t = pl.program_id(1)
    n_blocks = km_ref.shape[2]
    own_rows = pl.ds(pl.multiple_of(t * BLOCK, BLOCK), BLOCK)
    chains = [(a, slice(h * HEAD_DIM, (h + 1) * HEAD_DIM))
              for a in range(q_ref.shape[0]) for h in range(N_HEADS)]

    def kv_block_copies(group, block):
        rows = pl.ds(pl.multiple_of(block * BLOCK, BLOCK), BLOCK)
        return (pltpu.make_async_copy(k_hbm.at[:, group, rows, :], k_ref.at[:, rows, :], kv_sem.at[0]),
                pltpu.make_async_copy(vt_hbm.at[:, group, block], vt_ref.at[:, block], kv_sem.at[1]))

    @pl.when(jnp.logical_and(g == 0, t == 0))
    def _():
        for cp in kv_block_copies(0, 0):
            cp.start()

    for cp in kv_block_copies(g, t):
        cp.wait()

    @pl.when(t + 1 < n_blocks)
    def _():
        for cp in kv_block_copies(g, t + 1):
            cp.start()

    def score_t(i, rows):
        a, c = chains[i]
        return lax.dot_general(k_ref[a, rows, c], q_ref[a, 0, :, c], _NT, preferred_element_type=_F32)

    gates = []
    for a, c in chains:
        km = km_ref[a, 0, :, c]
        km_hi = km.astype(_BF16)
        km_lo = (km - km_hi.astype(_F32)).astype(_BF16)
        gates.append(lax.dot_general(km_hi, q_ref[a, 0, :, c], _NT, preferred_element_type=_F32)
                     + lax.dot_general(km_lo, q_ref[a, 0, :, c], _NT, preferred_element_type=_F32))
    own_scores = [score_t(i, own_rows) for i in range(len(chains))]

    blk = lax.broadcasted_iota(jnp.int32, (n_blocks, BLOCK), 0)
    for i, gate in enumerate(gates):
        rank = jnp.zeros(gate.shape, _F32)
        for m in range(n_blocks):
            g_m = gate[m:m + 1, :]
            beats = jnp.where(g_m > gate, 1.0, jnp.where(g_m == gate, jnp.where(blk > m, 1.0, 0.0), 0.0))
            rank = rank + jnp.where(m < t, beats, 0.0)
        selected = jnp.where(blk < t, rank, float(TOPK)) < TOPK
        bias_ref[i] = jnp.where(selected, 0.0, -jnp.inf)

    ones_rows = jnp.ones((_BF16_SUBLANES, BLOCK), _BF16)

    kpos = lax.broadcasted_iota(jnp.int32, (BLOCK, BLOCK), 0)
    qpos = lax.broadcasted_iota(jnp.int32, (BLOCK, BLOCK), 1)
    for i, (a, c) in enumerate(chains):
        s = jnp.where(kpos <= qpos, own_scores[i], -jnp.inf)
        m0 = jnp.max(s, axis=0, keepdims=True)
        p = jnp.exp2(s - m0).astype(_BF16)
        pv = _dot(jnp.concatenate([vt_ref[a, t, c, :], ones_rows], axis=0), p)
        m_ref[i:i + 1, :] = m0
        l_ref[i:i + 1, :] = pv[HEAD_DIM:HEAD_DIM + 1]
        acc_ref[i] = pv[:HEAD_DIM]

    def past_blocks(n0, count):
        rows = pl.ds(pl.multiple_of(n0 * BLOCK, BLOCK), count * BLOCK)
        scores = {i: score_t(i, rows) for i in range(MOBA_SCORE_LOOKAHEAD)}
        for i, (a, c) in enumerate(chains):
            s = scores.pop(i)
            s_blk = [s[b * BLOCK:(b + 1) * BLOCK] for b in range(count)]
            bias = [bias_ref[i, pl.ds(n0 + b, 1), :] for b in range(count)]
            m_run = m_ref[i:i + 1, :]
            m_new = m_run
            for b in range(count):
                m_new = jnp.maximum(m_new, jnp.max(s_blk[b], axis=0, keepdims=True) + bias[b])
            m_ref[i:i + 1, :] = m_new
            alpha = jnp.exp2(m_run - m_new)
            pv = None
            for b in range(count):
                p = jnp.exp2(s_blk[b] - (m_new - bias[b])).astype(_BF16)
                term = _dot(jnp.concatenate([vt_ref[a, n0 + b, c, :], ones_rows], axis=0), p)
                pv = term if pv is None else pv + term
            if i + MOBA_SCORE_LOOKAHEAD < len(chains):
                scores[i + MOBA_SCORE_LOOKAHEAD] = score_t(i + MOBA_SCORE_LOOKAHEAD, rows)
            acc_ref[i] = alpha * acc_ref[i] + pv[:HEAD_DIM]
            l_ref[i:i + 1, :] = alpha * l_ref[i:i + 1, :] + pv[HEAD_DIM:HEAD_DIM + 1]

    def past_block_pair(j, carry):
        past_blocks(2 * j, 2)
        return carry

    lax.fori_loop(0, lax.shift_right_logical(t, 1), past_block_pair, 0)

    @pl.when(lax.rem(t, 2) == 1)
    def _():
        past_blocks(t - 1, 1)

    for i, (a, c) in enumerate(chains):
        out = acc_ref[i] * (1.0 / l_ref[i:i + 1, :])
        o_ref[a, 0, :, c] = out.T.astype(_BF16)

    @pl.when(jnp.logical_and(t == n_blocks - 1, g + 1 < pl.num_programs(0)))
    def _():
        for cp in kv_block_copies(g + 1, 0):
            cp.start()


def _moba(q, k, vt, k_mean, bsz, seq):
    n_blocks = seq // BLOCK
    bps = MOBA_BATCHES_PER_STEP
    groups = bsz // bps
    n_chains = bps * N_HEADS
    q4 = q.reshape(bps, groups, seq, D_MODEL)
    k4 = k.reshape(bps, groups, seq, D_MODEL)
    vt5 = vt.reshape(bps, groups, n_blocks, D_MODEL, BLOCK)
    km4 = k_mean.reshape(bps, groups, n_blocks, D_MODEL)
    blk = bps * (2 * BLOCK * D_MODEL * 2 + n_blocks * D_MODEL * 4)
    scratch = 2 * bps * seq * D_MODEL * 2 + n_chains * (n_blocks + HEAD_DIM + 2) * BLOCK * 4
    out = pl.pallas_call(
        _moba_kernel,
        grid=(groups, n_blocks),
        in_specs=[
            pl.BlockSpec((bps, 1, BLOCK, D_MODEL), lambda g, t: (0, g, t, 0)),
            _HBM,
            _HBM,
            pl.BlockSpec((bps, 1, n_blocks, D_MODEL), lambda g, t: (0, g, 0, 0)),
        ],
        out_specs=pl.BlockSpec((bps, 1, BLOCK, D_MODEL), lambda g, t: (0, g, t, 0)),
        out_shape=jax.ShapeDtypeStruct((bps, groups, seq, D_MODEL), _BF16),
        scratch_shapes=[
            pltpu.VMEM((bps, seq, D_MODEL), _BF16),
            pltpu.VMEM((bps, n_blocks, D_MODEL, BLOCK), _BF16),
            pltpu.SemaphoreType.DMA((2,)),
            pltpu.VMEM((n_chains, n_blocks, BLOCK), _F32),
            pltpu.VMEM((n_chains, BLOCK), _F32),
            pltpu.VMEM((n_chains, BLOCK), _F32),
            pltpu.VMEM((n_chains, HEAD_DIM, BLOCK), _F32),
        ],
        compiler_params=_params(blk, scratch + n_chains * 6 * BLOCK * BLOCK * 4, 2),
        name="moba",
    )(q4, k4, vt5, km4)
    return out.reshape(bsz * seq, D_MODEL)


def kernel(x, mem, positions, norm_mix, norm_mem, norm_memkv, norm_ffn, norm_final,
           conv_w_in, conv_w, conv_w_out, kv_norm, w_kv, moba_w_q, moba_w_o,
           mem_w_q, mem_w_kv, mem_w_o, ffn_w_gu, ffn_w_down):
    bsz, seq, d = x.shape
    assert d == D_MODEL and seq % TOKEN_TILE == 0 and seq % QKV_TOKEN_TILE == 0
    assert TOKEN_TILE % SUB_TILE == 0 and QKV_TOKEN_TILE % SUB_TILE == 0 and SUB_TILE % BLOCK == 0
    assert TOKEN_TILE % FFN_SUB_TILE == 0
    assert mem.shape == (bsz, MEM_LEN, D_MODEL)
    assert bsz % MOBA_BATCHES_PER_STEP == 0
    assert norm_mix.shape[0] == DEPTH == 2 and conv_w_in.shape[0] == 1 and moba_w_q.shape[0] == 1

    x2d = x.reshape(bsz * seq, d)
    mem_qk, mem_vo = _memkv(mem, norm_memkv, mem_w_kv, mem_w_q, mem_w_o)

    x2d = _convmix(x2d, norm_mix, conv_w_in, conv_w, conv_w_out, seq)
    x2d = _memattn(x2d, norm_mem, mem_qk, mem_vo, 0, seq)
    x2d = _swiglu(x2d, norm_ffn, ffn_w_gu, ffn_w_down, 0)

    q, k, vt, k_mean = _qkv(x2d, positions, norm_mix, kv_norm, moba_w_q, w_kv, bsz, seq)
    attn = _moba(q, k, vt, k_mean, bsz, seq)

    x2d = _memattn(x2d, norm_mem, mem_qk, mem_vo, 1, seq, pre=(attn, moba_w_o))
    x2d = _swiglu(x2d, norm_ffn, ffn_w_gu, ffn_w_down, 1, final_gain=norm_final)
    return x2d.reshape(bsz, seq, d)
```

```python
import functools
import math

import jax
import jax.numpy as jnp
from jax import lax
from jax.experimental import pallas as pl
from jax.experimental.pallas import tpu as pltpu

D_MODEL = 1024
DEPTH = 2
CONV_W = 3
N_HEADS = 8
HEAD_DIM = D_MODEL // N_HEADS
BLOCK = 256
TOPK = 3
ROT_DIM = HEAD_DIM // 4
ROPE_THETA = 500000.0
MEM_LEN = 256
MEM_HEADS = 4
MEM_HEAD_DIM = D_MODEL // MEM_HEADS
D_FF = ((8 * D_MODEL + 3 * 256 - 1) // (3 * 256)) * 256
EPS = 1e-6

V7X_VMEM_BYTES = 64 * 1024 * 1024
V7X_LANES = 128
V7X_SUBLANES = 8
V7X_MXU_DIM = 256
_BF16_SUBLANES = 2 * V7X_SUBLANES

TOKEN_TILE = 1024
QKV_TOKEN_TILE = 512
SUB_TILE = 512
FFN_SUB_TILE = 256
FF_CHUNK = 4 * V7X_MXU_DIM
MOBA_BATCHES_PER_STEP = 2
MOBA_SCORE_LOOKAHEAD = 4
STAGE_CHUNK_BYTES = 2 * 1024 * 1024
STAGE_SLOTS = 3
STAGE_BYTES = STAGE_SLOTS * STAGE_CHUNK_BYTES

_F32 = jnp.float32
_BF16 = jnp.bfloat16
_NT = (((1,), (1,)), ((), ()))
_HBM = pl.BlockSpec(memory_space=pl.ANY)


def _vmem_limit(block_bytes, resident_bytes):
    need = 2 * block_bytes + resident_bytes
    return int(min(need + need // 4, V7X_VMEM_BYTES - 8 * 1024 * 1024))


def _params(block_bytes, resident_bytes, ndim):
    return pltpu.CompilerParams(
        dimension_semantics=("arbitrary",) * ndim,
        vmem_limit_bytes=_vmem_limit(block_bytes, resident_bytes))


def _dot(a, b):
    return jnp.dot(a, b, preferred_element_type=_F32)


def _sub_tiles(rows, sub=None):
    sub = sub or SUB_TILE
    return [slice(r0, r0 + sub) for r0 in range(0, rows, sub)]


def _rms_unit(x):
    ms = jnp.mean(x * x, axis=-1, keepdims=True)
    return x * lax.rsqrt(ms + EPS)


def _gain_spec(layer):
    return pl.BlockSpec((1, 1, D_MODEL), lambda *_: (layer, 0, 0))


def _stage_rows(rows, cols):
    best = V7X_SUBLANES
    for cand in range(V7X_SUBLANES, rows + 1, V7X_SUBLANES):
        if rows % cand == 0 and cand * cols * 4 <= STAGE_CHUNK_BYTES:
            best = cand
    return best


def _stage_weights(first_step, jobs):
    @pl.when(first_step)
    def _():
        for src, dst in jobs:
            rows, cols = dst.shape
            chunk = _stage_rows(rows, cols)
            n_chunks = rows // chunk
            ahead = STAGE_SLOTS - 1

            def load(stage, sem):
                def copy(c):
                    slot = c % STAGE_SLOTS
                    return pltpu.make_async_copy(src.at[pl.ds(c * chunk, chunk)], stage.at[slot], sem.at[slot])

                for c in range(min(ahead, n_chunks)):
                    copy(c).start()
                for c in range(n_chunks):
                    if c + ahead < n_chunks:
                        copy(c + ahead).start()
                    copy(c).wait()
                    dst[c * chunk:(c + 1) * chunk, :] = stage[c % STAGE_SLOTS].astype(_BF16)

            pl.run_scoped(load, pltpu.VMEM((STAGE_SLOTS, chunk, cols), _F32),
                          pltpu.SemaphoreType.DMA((STAGE_SLOTS,)))


def _memkv_kernel(mem_ref, g_ref, wkv_hbm, wq_hbm, wo_hbm, qk_ref, vo_ref, wkv_ref, wq_ref, wo_ref):
    layer = pl.program_id(0)
    _stage_weights(pl.program_id(1) == 0, [(wkv_hbm.at[layer], wkv_ref), (wq_hbm.at[layer], wq_ref),
                                           (wo_hbm.at[layer], wo_ref)])
    h = (_rms_unit(mem_ref[0]) * g_ref[0]).astype(_BF16)
    kv = _dot(h, wkv_ref[...]).astype(_BF16)
    for hd in range(MEM_HEADS):
        c0 = hd * MEM_HEAD_DIM
        m0 = hd * MEM_LEN
        k_h = kv[:, c0:c0 + MEM_HEAD_DIM]
        v_h = kv[:, D_MODEL + c0:D_MODEL + c0 + MEM_HEAD_DIM]
        qk = lax.dot_general(wq_ref[:, c0:c0 + MEM_HEAD_DIM], k_h, _NT, preferred_element_type=_F32)
        qk_ref[0, 0, :, m0:m0 + MEM_LEN] = (qk * (MEM_HEAD_DIM ** -0.5)).astype(_BF16)
        vo_ref[0, 0, m0:m0 + MEM_LEN, :] = _dot(v_h, wo_ref[c0:c0 + MEM_HEAD_DIM, :]).astype(_BF16)


def _memkv(mem, norm_memkv, w_kv, w_q, w_o):
    bsz = mem.shape[0]
    cols = MEM_HEADS * MEM_LEN
    blk = MEM_LEN * D_MODEL * 4 + 2 * D_MODEL * cols * 2
    resident = 4 * D_MODEL * D_MODEL * 2 + STAGE_BYTES + 6 * MEM_LEN * 2 * D_MODEL * 4
    w_scratch = pltpu.VMEM((D_MODEL, D_MODEL), _BF16)
    return pl.pallas_call(
        _memkv_kernel,
        grid=(DEPTH, bsz),
        in_specs=[
            pl.BlockSpec((1, MEM_LEN, D_MODEL), lambda l, b: (b, 0, 0)),
            pl.BlockSpec((1, 1, D_MODEL), lambda l, b: (l, 0, 0)),
            _HBM, _HBM, _HBM,
        ],
        out_specs=[
            pl.BlockSpec((1, 1, D_MODEL, cols), lambda l, b: (l, b, 0, 0)),
            pl.BlockSpec((1, 1, cols, D_MODEL), lambda l, b: (l, b, 0, 0)),
        ],
        out_shape=[
            jax.ShapeDtypeStruct((DEPTH, bsz, D_MODEL, cols), _BF16),
            jax.ShapeDtypeStruct((DEPTH, bsz, cols, D_MODEL), _BF16),
        ],
        scratch_shapes=[pltpu.VMEM((D_MODEL, 2 * D_MODEL), _BF16), w_scratch, w_scratch],
        compiler_params=_params(blk, resident, 2),
        name="memkv",
    )(mem, norm_memkv.reshape(DEPTH, 1, D_MODEL), w_kv, w_q, w_o)


def _shift_rows(z, prev, k):
    zk = pltpu.roll(z, k, 0)
    pk = pltpu.roll(prev, k, 0)
    r = lax.broadcasted_iota(jnp.int32, prev.shape, 0)
    top = jnp.where(r < k, pk, zk[:V7X_SUBLANES])
    return jnp.concatenate([top, zk[V7X_SUBLANES:]], axis=0)


def _convmix_kernel(x_ref, g_ref, win_hbm, cw_ref, wout_hbm, o_ref, win_ref, wout_ref, carry_ref,
                    *, tiles_per_seq):
    d = D_MODEL
    _stage_weights(pl.program_id(0) == 0, [(win_hbm.at[0], win_ref), (wout_hbm.at[0], wout_ref)])

    @pl.when(pl.program_id(0) % tiles_per_seq == 0)
    def _():
        carry_ref[...] = jnp.zeros_like(carry_ref)

    for rows in _sub_tiles(x_ref.shape[0]):
        x = x_ref[rows, :]
        h = (_rms_unit(x) * g_ref[0]).astype(_BF16)
        b_gate = _dot(h, win_ref[:, 0:d])
        z = _dot(h, win_ref[:, d:2 * d]) * _dot(h, win_ref[:, 2 * d:3 * d])
        prev = carry_ref[...]
        carry_ref[...] = z[z.shape[0] - V7X_SUBLANES:, :]
        conv = (cw_ref[0, 2:3, :] * z
                + cw_ref[0, 1:2, :] * _shift_rows(z, prev, 1)
                + cw_ref[0, 0:1, :] * _shift_rows(z, prev, 2))
        y = (b_gate * conv).astype(_BF16)
        o_ref[rows, :] = x + _dot(y, wout_ref[...])


def _convmix(x2d, norm_mix, w_in, w_conv, w_out, seq):
    n = x2d.shape[0]
    tm = TOKEN_TILE
    blk = 2 * tm * D_MODEL * 4
    resident = 4 * D_MODEL * D_MODEL * 2 + STAGE_BYTES + 10 * SUB_TILE * D_MODEL * 4
    return pl.pallas_call(
        functools.partial(_convmix_kernel, tiles_per_seq=seq // tm),
        grid=(n // tm,),
        in_specs=[
            pl.BlockSpec((tm, D_MODEL), lambda i: (i, 0)),
            _gain_spec(0),
            _HBM,
            pl.BlockSpec((1, CONV_W, D_MODEL), lambda i: (0, 0, 0)),
            _HBM,
        ],
        out_specs=pl.BlockSpec((tm, D_MODEL), lambda i: (i, 0)),
        out_shape=jax.ShapeDtypeStruct((n, D_MODEL), _F32),
        scratch_shapes=[
            pltpu.VMEM((D_MODEL, 3 * D_MODEL), _BF16),
            pltpu.VMEM((D_MODEL, D_MODEL), _BF16),
            pltpu.VMEM((V7X_SUBLANES, D_MODEL), _F32),
        ],
        compiler_params=_params(blk, resident, 1),
        name="convmix",
    )(x2d, norm_mix.reshape(DEPTH, 1, D_MODEL), w_in, w_conv, w_out)


def _memattn_kernel(*refs, has_pre):
    if has_pre:
        x_ref, a_ref, wpre_hbm, g_ref, qk_ref, vo_ref, o_ref, wpre_ref = refs
        _stage_weights(pl.program_id(0) == 0, [(wpre_hbm.at[0], wpre_ref)])
    else:
        x_ref, g_ref, qk_ref, vo_ref, o_ref = refs
    for rows in _sub_tiles(x_ref.shape[0]):
        x = x_ref[rows, :]
        if has_pre:
            x = x + _dot(a_ref[rows, :], wpre_ref[...])
        h = (_rms_unit(x) * g_ref[0]).astype(_BF16)
        scores = _dot(h, qk_ref[0, 0])
        probs = []
        for hd in range(MEM_HEADS):
            s = scores[:, hd * MEM_LEN:(hd + 1) * MEM_LEN]
            p = jnp.exp(s - jnp.max(s, axis=-1, keepdims=True))
            probs.append((p * (1.0 / jnp.sum(p, axis=-1, keepdims=True))).astype(_BF16))
        o_ref[rows, :] = x + _dot(jnp.concatenate(probs, axis=1), vo_ref[0, 0])


def _memattn(x2d, norm_mem, qk, vo, layer, seq, pre=None):
    n = x2d.shape[0]
    tm = TOKEN_TILE
    tiles_per_seq = seq // tm
    cols = MEM_HEADS * MEM_LEN
    row_spec = pl.BlockSpec((tm, D_MODEL), lambda i: (i, 0))
    in_specs = [row_spec]
    args = [x2d]
    scratch = []
    if pre is not None:
        in_specs += [row_spec, _HBM]
        args += list(pre)
        scratch.append(pltpu.VMEM((D_MODEL, D_MODEL), _BF16))
    in_specs += [
        _gain_spec(layer),
        pl.BlockSpec((1, 1, D_MODEL, cols), lambda i: (layer, i // tiles_per_seq, 0, 0)),
        pl.BlockSpec((1, 1, cols, D_MODEL), lambda i: (layer, i // tiles_per_seq, 0, 0)),
    ]
    args += [norm_mem.reshape(DEPTH, 1, D_MODEL), qk, vo]
    blk = 3 * tm * D_MODEL * 4 + 2 * D_MODEL * cols * 2
    resident = len(scratch) * D_MODEL * D_MODEL * 2 + STAGE_BYTES + 8 * SUB_TILE * D_MODEL * 4
    return pl.pallas_call(
        functools.partial(_memattn_kernel, has_pre=pre is not None),
        grid=(n // tm,),
        in_specs=in_specs,
        out_specs=row_spec,
        out_shape=jax.ShapeDtypeStruct((n, D_MODEL), _F32),
        scratch_shapes=scratch,
        compiler_params=_params(blk, resident, 1),
        name="memattn_l%d" % layer,
    )(*args)


def _ff_chunks():
    return [(c0, min(FF_CHUNK, D_FF - c0)) for c0 in range(0, D_FF, FF_CHUNK)]


def _swiglu_kernel(*refs, final, layer):
    if final:
        x_ref, g_ref, wgu_hbm, wd_hbm, gf_ref, o_ref, wgu_ref, wd_ref = refs
    else:
        x_ref, g_ref, wgu_hbm, wd_hbm, o_ref, wgu_ref, wd_ref = refs
    _stage_weights(pl.program_id(0) == 0, [(wgu_hbm.at[layer], wgu_ref), (wd_hbm.at[layer], wd_ref)])
    for rows in _sub_tiles(x_ref.shape[0], FFN_SUB_TILE):
        x = x_ref[rows, :]
        h = (_rms_unit(x) * g_ref[0]).astype(_BF16)
        acc = x
        for c0, cw in _ff_chunks():
            gate = _dot(h, wgu_ref[:, c0:c0 + cw])
            up = _dot(h, wgu_ref[:, D_FF + c0:D_FF + c0 + cw])
            act = (gate * (1.0 / (1.0 + jnp.exp(-gate))) * up).astype(_BF16)
            acc = acc + _dot(act, wd_ref[c0:c0 + cw, :])
        if final:
            acc = _rms_unit(acc) * gf_ref[0]
        o_ref[rows, :] = acc


def _swiglu(x2d, norm_ffn, w_gu, w_down, layer, final_gain=None):
    n = x2d.shape[0]
    tm = TOKEN_TILE
    in_specs = [pl.BlockSpec((tm, D_MODEL), lambda i: (i, 0)), _gain_spec(layer), _HBM, _HBM]
    args = [x2d, norm_ffn.reshape(DEPTH, 1, D_MODEL), w_gu, w_down]
    if final_gain is not None:
        in_specs.append(_gain_spec(0))
        args.append(final_gain.reshape(1, 1, D_MODEL))
    blk = 2 * tm * D_MODEL * 4
    resident = (3 * D_MODEL * D_FF * 2 + STAGE_BYTES
                + 4 * FFN_SUB_TILE * D_MODEL * 4 + 4 * FFN_SUB_TILE * FF_CHUNK * 4)
    return pl.pallas_call(
        functools.partial(_swiglu_kernel, final=final_gain is not None, layer=layer),
        grid=(n // tm,),
        in_specs=in_specs,
        out_specs=pl.BlockSpec((tm, D_MODEL), lambda i: (i, 0)),
        out_shape=jax.ShapeDtypeStruct((n, D_MODEL), _F32),
        scratch_shapes=[pltpu.VMEM((D_MODEL, 2 * D_FF), _BF16), pltpu.VMEM((D_FF, D_MODEL), _BF16)],
        compiler_params=_params(blk, resident, 1),
        name="swiglu_final" if final_gain is not None else "swiglu",
    )(*args)


def _rope(x, cos, sin_signed, first_half):
    d = x.shape[1]
    rot = jnp.where(first_half, pltpu.roll(x, d - ROT_DIM // 2, 1), pltpu.roll(x, ROT_DIM // 2, 1))
    return x * cos + rot * sin_signed


def _qkv_kernel(x_ref, pos_ref, invf_ref, gq_ref, gk_ref, wq_hbm, wkv_hbm,
                q_ref, k_ref, vt_ref, km_ref, wq_ref, wkv_ref, *, tiles_per_seq):
    d = D_MODEL
    tm = x_ref.shape[0]
    blocks_per_tile = tm // BLOCK
    s_tile = pl.program_id(0) % tiles_per_seq
    _stage_weights(pl.program_id(0) == 0, [(wq_hbm.at[0], wq_ref), (wkv_hbm, wkv_ref)])

    @pl.when(s_tile == 0)
    def _():
        km_ref[...] = jnp.zeros_like(km_ref)

    lane = lax.broadcasted_iota(jnp.int32, (1, HEAD_DIM), 1)
    lane_d = lax.broadcasted_iota(jnp.int32, (1, d), 1)
    first_half = (lane_d % HEAD_DIM) < ROT_DIM // 2
    row16 = lax.broadcasted_iota(jnp.int32, km_ref.shape[1:], 0)

    for rows in _sub_tiles(tm):
        xh = _rms_unit(x_ref[rows, :])
        hq = (xh * gq_ref[0]).astype(_BF16)
        hk = (xh * gk_ref[0]).astype(_BF16)

        pos = pos_ref[0, :, rows].astype(_F32)
        pos_rows = jnp.broadcast_to(pos, (V7X_LANES, SUB_TILE)).T
        ang = pos_rows * invf_ref[...]
        cos = jnp.cos(ang)
        sin_signed = jnp.where(lane < ROT_DIM // 2, -1.0, 1.0) * jnp.sin(ang)
        cos = jnp.concatenate([cos] * N_HEADS, axis=1)
        sin_signed = jnp.concatenate([sin_signed] * N_HEADS, axis=1)

        q = _dot(hq, wq_ref[...]) * (HEAD_DIM ** -0.5 * math.log2(math.e))
        q_ref[rows, :] = _rope(q, cos, sin_signed, first_half).astype(_BF16)

        k = _rope(_dot(hk, wkv_ref[:, 0:d]), cos, sin_signed, first_half)
        k_ref[rows, :] = k.astype(_BF16)
        v = _dot(hk, wkv_ref[:, d:2 * d])

        for r0 in range(0, SUB_TILE, BLOCK):
            j = (rows.start + r0) // BLOCK
            vt_ref[0, j] = v[r0:r0 + BLOCK, :].T.astype(_BF16)
            mean_row = jnp.sum(k[r0:r0 + BLOCK, :], axis=0, keepdims=True) * (1.0 / BLOCK)
            km_ref[0] = jnp.where(row16 == s_tile * blocks_per_tile + j, mean_row, km_ref[0])


def _qkv(x2d, positions, norm_mix, kv_norm, w_q, w_kv, bsz, seq):
    n = x2d.shape[0]
    tm = QKV_TOKEN_TILE
    tiles_per_seq = seq // tm
    n_blocks = seq // BLOCK
    inv_freq = ROPE_THETA ** (-jnp.arange(0, ROT_DIM, 2, dtype=_F32) / ROT_DIM)
    invf_lane = jnp.concatenate(
        [inv_freq, inv_freq, jnp.zeros((HEAD_DIM - ROT_DIM,), _F32)]).reshape(1, HEAD_DIM)
    row_spec = pl.BlockSpec((tm, D_MODEL), lambda i: (i, 0))
    blk = tm * D_MODEL * 4 + 3 * tm * D_MODEL * 2 + n_blocks * D_MODEL * 4
    resident = 3 * D_MODEL * D_MODEL * 2 + STAGE_BYTES + 12 * SUB_TILE * D_MODEL * 4
    return pl.pallas_call(
        functools.partial(_qkv_kernel, tiles_per_seq=tiles_per_seq),
        grid=(n // tm,),
        in_specs=[
            row_spec,
            pl.BlockSpec((1, 1, tm), lambda i: (i, 0, 0)),
            pl.BlockSpec((1, HEAD_DIM), lambda i: (0, 0)),
            _gain_spec(1),
            _gain_spec(0),
            _HBM,
            _HBM,
        ],
        out_specs=[
            row_spec,
            row_spec,
            pl.BlockSpec((1, tm // BLOCK, D_MODEL, BLOCK),
                         lambda i: (i // tiles_per_seq, i % tiles_per_seq, 0, 0)),
            pl.BlockSpec((1, n_blocks, D_MODEL), lambda i: (i // tiles_per_seq, 0, 0)),
        ],
        out_shape=[
            jax.ShapeDtypeStruct((n, D_MODEL), _BF16),
            jax.ShapeDtypeStruct((n, D_MODEL), _BF16),
            jax.ShapeDtypeStruct((bsz, n_blocks, D_MODEL, BLOCK), _BF16),
            jax.ShapeDtypeStruct((bsz, n_blocks, D_MODEL), _F32),
        ],
        scratch_shapes=[pltpu.VMEM((D_MODEL, D_MODEL), _BF16), pltpu.VMEM((D_MODEL, 2 * D_MODEL), _BF16)],
        compiler_params=_params(blk, resident, 1),
        name="qkv",
    )(x2d, positions.reshape(n // tm, 1, tm), invf_lane,
      norm_mix.reshape(DEPTH, 1, D_MODEL), kv_norm.reshape(1, 1, D_MODEL), w_q, w_kv)


def _moba_kernel(q_ref, k_hbm, vt_hbm, km_ref, o_ref, k_ref, vt_ref, kv_sem, bias_ref, m_ref, l_ref, acc_ref):
    g = pl.program_id(0)
    t = pl.program_id(1)
    n_blocks = km_ref.shape[2]
    own_rows = pl.ds(pl.multiple_of(t * BLOCK, BLOCK), BLOCK)
    chains = [(a, slice(h * HEAD_DIM, (h + 1) * HEAD_DIM))
              for a in range(q_ref.shape[0]) for h in range(N_HEADS)]

    def kv_block_copies(group, block):
        rows = pl.ds(pl.multiple_of(block * BLOCK, BLOCK), BLOCK)
        return (pltpu.make_async_copy(k_hbm.at[:, group, rows, :], k_ref.at[:, rows, :], kv_sem.at[0]),
                pltpu.make_async_copy(vt_hbm.at[:, group, block], vt_ref.at[:, block], kv_sem.at[1]))

    @pl.when(jnp.logical_and(g == 0, t == 0))
    def _():
        for cp in kv_block_copies(0, 0):
            cp.start()

    for cp in kv_block_copies(g, t):
        cp.wait()

    @pl.when(t + 1 < n_blocks)
    def _():
        for cp in kv_block_copies(g, t + 1):
            cp.start()

    def score_t(i, rows):
        a, c = chains[i]
        return lax.dot_general(k_ref[a, rows, c], q_ref[a, 0, :, c], _NT, preferred_element_type=_F32)

    gates = []
    for a, c in chains:
        km = km_ref[a, 0, :, c]
        km_hi = km.astype(_BF16)
        km_lo = (km - km_hi.astype(_F32)).astype(_BF16)
        gates.append(lax.dot_general(km_hi, q_ref[a, 0, :, c], _NT, preferred_element_type=_F32)
                     + lax.dot_general(km_lo, q_ref[a, 0, :, c], _NT, preferred_element_type=_F32))
    own_scores = [score_t(i, own_rows) for i in range(len(chains))]

    blk = lax.broadcasted_iota(jnp.int32, (n_blocks, BLOCK), 0)
    for i, gate in enumerate(gates):
        rank = jnp.zeros(gate.shape, _F32)
        for m in range(n_blocks):
            g_m = gate[m:m + 1, :]
            beats = jnp.where(g_m > gate, 1.0, jnp.where(g_m == gate, jnp.where(blk > m, 1.0, 0.0), 0.0))
            rank = rank + jnp.where(m < t, beats, 0.0)
        selected = jnp.where(blk < t, rank, float(TOPK)) < TOPK
        bias_ref[i] = jnp.where(selected, 0.0, -jnp.inf)

    ones_rows = jnp.ones((_BF16_SUBLANES, BLOCK), _BF16)

    kpos = lax.broadcasted_iota(jnp.int32, (BLOCK, BLOCK), 0)
    qpos = lax.broadcasted_iota(jnp.int32, (BLOCK, BLOCK), 1)
    for i, (a, c) in enumerate(chains):
        s = jnp.where(kpos <= qpos, own_scores[i], -jnp.inf)
        m0 = jnp.max(s, axis=0, keepdims=True)
        p = jnp.exp2(s - m0).astype(_BF16)
        pv = _dot(jnp.concatenate([vt_ref[a, t, c, :], ones_rows], axis=0), p)
        m_ref[i:i + 1, :] = m0
        l_ref[i:i + 1, :] = pv[HEAD_DIM:HEAD_DIM + 1]
        acc_ref[i] = pv[:HEAD_DIM]

    def past_blocks(n0, count):
        rows = pl.ds(pl.multiple_of(n0 * BLOCK, BLOCK), count * BLOCK)
        scores = {i: score_t(i, rows) for i in range(MOBA_SCORE_LOOKAHEAD)}
        for i, (a, c) in enumerate(chains):
            s = scores.pop(i)
            s_blk = [s[b * BLOCK:(b + 1) * BLOCK] for b in range(count)]
            bias = [bias_ref[i, pl.ds(n0 + b, 1), :] for b in range(count)]
            m_run = m_ref[i:i + 1, :]
            m_new = m_run
            for b in range(count):
                m_new = jnp.maximum(m_new, jnp.max(s_blk[b], axis=0, keepdims=True) + bias[b])
            m_ref[i:i + 1, :] = m_new
            alpha = jnp.exp2(m_run - m_new)
            pv = None
            for b in range(count):
                p = jnp.exp2(s_blk[b] - (m_new - bias[b])).astype(_BF16)
                term = _dot(jnp.concatenate([vt_ref[a, n0 + b, c, :], ones_rows], axis=0), p)
                pv = term if pv is None else pv + term
            if i + MOBA_SCORE_LOOKAHEAD < len(chains):
                scores[i + MOBA_SCORE_LOOKAHEAD] = score_t(i + MOBA_SCORE_LOOKAHEAD, rows)
            acc_ref[i] = alpha * acc_ref[i] + pv[:HEAD_DIM]
            l_ref[i:i + 1, :] = alpha * l_ref[i:i + 1, :] + pv[HEAD_DIM:HEAD_DIM + 1]

    def past_block_pair(j, carry):
        past_blocks(2 * j, 2)
        return carry

    lax.fori_loop(0, lax.shift_right_logical(t, 1), past_block_pair, 0)

    @pl.when(lax.rem(t, 2) == 1)
    def _():
        past_blocks(t - 1, 1)

    for i, (a, c) in enumerate(chains):
        out = acc_ref[i] * (1.0 / l_ref[i:i + 1, :])
        o_ref[a, 0, :, c] = out.T.astype(_BF16)

    @pl.when(jnp.logical_and(t == n_blocks - 1, g + 1 < pl.num_programs(0)))
    def _():
        for cp in kv_block_copies(g + 1, 0):
            cp.start()


def _moba(q, k, vt, k_mean, bsz, seq):
    n_blocks = seq // BLOCK
    bps = MOBA_BATCHES_PER_STEP
    groups = bsz // bps
    n_chains = bps * N_HEADS
    q4 = q.reshape(bps, groups, seq, D_MODEL)
    k4 = k.reshape(bps, groups, seq, D_MODEL)
    vt5 = vt.reshape(bps, groups, n_blocks, D_MODEL, BLOCK)
    km4 = k_mean.reshape(bps, groups, n_blocks, D_MODEL)
    blk = bps * (2 * BLOCK * D_MODEL * 2 + n_blocks * D_MODEL * 4)
    scratch = 2 * bps * seq * D_MODEL * 2 + n_chains * (n_blocks + HEAD_DIM + 2) * BLOCK * 4
    out = pl.pallas_call(
        _moba_kernel,
        grid=(groups, n_blocks),
        in_specs=[
            pl.BlockSpec((bps, 1, BLOCK, D_MODEL), lambda g, t: (0, g, t, 0)),
            _HBM,
            _HBM,
            pl.BlockSpec((bps, 1, n_blocks, D_MODEL), lambda g, t: (0, g, 0, 0)),
        ],
        out_specs=pl.BlockSpec((bps, 1, BLOCK, D_MODEL), lambda g, t: (0, g, t, 0)),
        out_shape=jax.ShapeDtypeStruct((bps, groups, seq, D_MODEL), _BF16),
        scratch_shapes=[
            pltpu.VMEM((bps, seq, D_MODEL), _BF16),
            pltpu.VMEM((bps, n_blocks, D_MODEL, BLOCK), _BF16),
            pltpu.SemaphoreType.DMA((2,)),
            pltpu.VMEM((n_chains, n_blocks, BLOCK), _F32),
            pltpu.VMEM((n_chains, BLOCK), _F32),
            pltpu.VMEM((n_chains, BLOCK), _F32),
            pltpu.VMEM((n_chains, HEAD_DIM, BLOCK), _F32),
        ],
        compiler_params=_params(blk, scratch + n_chains * 6 * BLOCK * BLOCK * 4, 2),
        name="moba",
    )(q4, k4, vt5, km4)
    return out.reshape(bsz * seq, D_MODEL)


def kernel(x, mem, positions, norm_mix, norm_mem, norm_memkv, norm_ffn, norm_final,
           conv_w_in, conv_w, conv_w_out, kv_norm, w_kv, moba_w_q, moba_w_o,
           mem_w_q, mem_w_kv, mem_w_o, ffn_w_gu, ffn_w_down):
    bsz, seq, d = x.shape
    assert d == D_MODEL and seq % TOKEN_TILE == 0 and seq % QKV_TOKEN_TILE == 0
    assert TOKEN_TILE % SUB_TILE == 0 and QKV_TOKEN_TILE % SUB_TILE == 0 and SUB_TILE % BLOCK == 0
    assert TOKEN_TILE % FFN_SUB_TILE == 0
    assert mem.shape == (bsz, MEM_LEN, D_MODEL)
    assert bsz % MOBA_BATCHES_PER_STEP == 0
    assert norm_mix.shape[0] == DEPTH == 2 and conv_w_in.shape[0] == 1 and moba_w_q.shape[0] == 1

    x2d = x.reshape(bsz * seq, d)
    mem_qk, mem_vo = _memkv(mem, norm_memkv, mem_w_kv, mem_w_q, mem_w_o)

    x2d = _convmix(x2d, norm_mix, conv_w_in, conv_w, conv_w_out, seq)
    x2d = _memattn(x2d, norm_mem, mem_qk, mem_vo, 0, seq)
    x2d = _swiglu(x2d, norm_ffn, ffn_w_gu, ffn_w_down, 0)

    q, k, vt, k_mean = _qkv(x2d, positions, norm_mix, kv_norm, moba_w_q, w_kv, bsz, seq)
    attn = _moba(q, k, vt, k_mean, bsz, seq)

    x2d = _memattn(x2d, norm_mem, mem_qk, mem_vo, 1, seq, pre=(attn, moba_w_o))
    x2d = _swiglu(x2d, norm_ffn, ffn_w_gu, ffn_w_down, 1, final_gain=norm_final)
    return x2d.reshape(bsz, seq, d)
```

```python
import functools
import math

import jax
import jax.numpy as jnp
from jax import lax
from jax.experimental import pallas as pl
from jax.experimental.pallas import tpu as pltpu

D_MODEL = 1024
DEPTH = 2
CONV_W = 3
N_HEADS = 8
HEAD_DIM = D_MODEL // N_HEADS
BLOCK = 256
TOPK = 3
ROT_DIM = HEAD_DIM // 4
ROPE_THETA = 500000.0
MEM_LEN = 256
MEM_HEADS = 4
MEM_HEAD_DIM = D_MODEL // MEM_HEADS
D_FF = ((8 * D_MODEL + 3 * 256 - 1) // (3 * 256)) * 256
EPS = 1e-6

V7X_VMEM_BYTES = 64 * 1024 * 1024
V7X_VMEM_RESERVE_BYTES = 8 * 1024 * 1024
V7X_LANES = 128
V7X_SUBLANES = 8
V7X_MXU_DIM = 256
_BF16_SUBLANES = 2 * V7X_SUBLANES

TOKEN_TILE = 1024
QKV_TOKEN_TILE = 512
SUB_TILE = 512
FFN_SUB_TILE = 256
FF_CHUNK = 4 * V7X_MXU_DIM
MOBA_BATCHES_PER_STEP = 2
MOBA_SCORE_LOOKAHEAD = 5
STAGE_CHUNK_BYTES = 2 * 1024 * 1024
STAGE_SLOTS = 3
STAGE_BYTES = STAGE_SLOTS * STAGE_CHUNK_BYTES

_F32 = jnp.float32
_BF16 = jnp.bfloat16
_NT = (((1,), (1,)), ((), ()))
_HBM = pl.BlockSpec(memory_space=pl.ANY)


def _vmem_limit(block_bytes, resident_bytes):
    need = 2 * block_bytes + resident_bytes
    return int(min(need + need // 4, V7X_VMEM_BYTES - V7X_VMEM_RESERVE_BYTES))


def _params(block_bytes, resident_bytes, ndim):
    return pltpu.CompilerParams(
        dimension_semantics=("arbitrary",) * ndim,
        vmem_limit_bytes=_vmem_limit(block_bytes, resident_bytes))


def _dot(a, b):
    return jnp.dot(a, b, preferred_element_type=_F32)


def _sub_tiles(rows, sub=None):
    sub = sub or SUB_TILE
    return [slice(r0, r0 + sub) for r0 in range(0, rows, sub)]


def _rms_unit(x):
    ms = jnp.mean(x * x, axis=-1, keepdims=True)
    return x * lax.rsqrt(ms + EPS)


def _gain_spec(layer):
    return pl.BlockSpec((1, 1, D_MODEL), lambda *_: (layer, 0, 0))


def _stage_rows(rows, cols):
    best = V7X_SUBLANES
    for cand in range(V7X_SUBLANES, rows + 1, V7X_SUBLANES):
        if rows % cand == 0 and cand * cols * 4 <= STAGE_CHUNK_BYTES:
            best = cand
    return best


def _stage_weights(first_step, jobs):
    @pl.when(first_step)
    def _():
        for src, dst in jobs:
            rows, cols = dst.shape
            chunk = _stage_rows(rows, cols)
            n_chunks = rows // chunk
            ahead = STAGE_SLOTS - 1

            def load(stage, sem):
                def copy(c):
                    slot = c % STAGE_SLOTS
                    return pltpu.make_async_copy(src.at[pl.ds(c * chunk, chunk)], stage.at[slot], sem.at[slot])

                for c in range(min(ahead, n_chunks)):
                    copy(c).start()
                for c in range(n_chunks):
                    if c + ahead < n_chunks:
                        copy(c + ahead).start()
                    copy(c).wait()
                    dst[c * chunk:(c + 1) * chunk, :] = stage[c % STAGE_SLOTS].astype(_BF16)

            pl.run_scoped(load, pltpu.VMEM((STAGE_SLOTS, chunk, cols), _F32),
                          pltpu.SemaphoreType.DMA((STAGE_SLOTS,)))


def _memkv_kernel(mem_ref, g_ref, wkv_hbm, wq_hbm, wo_hbm, qk_ref, vo_ref, wkv_ref, wq_ref, wo_ref):
    layer = pl.program_id(0)
    _stage_weights(pl.program_id(1) == 0, [(wkv_hbm.at[layer], wkv_ref), (wq_hbm.at[layer], wq_ref),
                                           (wo_hbm.at[layer], wo_ref)])
    h = (_rms_unit(mem_ref[0]) * g_ref[0]).astype(_BF16)
    kv = _dot(h, wkv_ref[...]).astype(_BF16)
    for hd in range(MEM_HEADS):
        c0 = hd * MEM_HEAD_DIM
        m0 = hd * MEM_LEN
        k_h = kv[:, c0:c0 + MEM_HEAD_DIM]
        v_h = kv[:, D_MODEL + c0:D_MODEL + c0 + MEM_HEAD_DIM]
        qk = lax.dot_general(wq_ref[:, c0:c0 + MEM_HEAD_DIM], k_h, _NT, preferred_element_type=_F32)
        qk_ref[0, 0, :, m0:m0 + MEM_LEN] = (qk * (MEM_HEAD_DIM ** -0.5)).astype(_BF16)
        vo_ref[0, 0, m0:m0 + MEM_LEN, :] = _dot(v_h, wo_ref[c0:c0 + MEM_HEAD_DIM, :]).astype(_BF16)


def _memkv(mem, norm_memkv, w_kv, w_q, w_o):
    bsz = mem.shape[0]
    cols = MEM_HEADS * MEM_LEN
    blk = MEM_LEN * D_MODEL * 4 + 2 * D_MODEL * cols * 2
    resident = 4 * D_MODEL * D_MODEL * 2 + STAGE_BYTES + 6 * MEM_LEN * 2 * D_MODEL * 4
    w_scratch = pltpu.VMEM((D_MODEL, D_MODEL), _BF16)
    return pl.pallas_call(
        _memkv_kernel,
        grid=(DEPTH, bsz),
        in_specs=[
            pl.BlockSpec((1, MEM_LEN, D_MODEL), lambda l, b: (b, 0, 0)),
            pl.BlockSpec((1, 1, D_MODEL), lambda l, b: (l, 0, 0)),
            _HBM, _HBM, _HBM,
        ],
        out_specs=[
            pl.BlockSpec((1, 1, D_MODEL, cols), lambda l, b: (l, b, 0, 0)),
            pl.BlockSpec((1, 1, cols, D_MODEL), lambda l, b: (l, b, 0, 0)),
        ],
        out_shape=[
            jax.ShapeDtypeStruct((DEPTH, bsz, D_MODEL, cols), _BF16),
            jax.ShapeDtypeStruct((DEPTH, bsz, cols, D_MODEL), _BF16),
        ],
        scratch_shapes=[pltpu.VMEM((D_MODEL, 2 * D_MODEL), _BF16), w_scratch, w_scratch],
        compiler_params=_params(blk, resident, 2),
        name="memkv",
    )(mem, norm_memkv.reshape(DEPTH, 1, D_MODEL), w_kv, w_q, w_o)


def _shift_rows(z, prev, k):
    zk = pltpu.roll(z, k, 0)
    pk = pltpu.roll(prev, k, 0)
    r = lax.broadcasted_iota(jnp.int32, prev.shape, 0)
    top = jnp.where(r < k, pk, zk[:V7X_SUBLANES])
    return jnp.concatenate([top, zk[V7X_SUBLANES:]], axis=0)


def _mem_attention(x, gain, qk, vo):
    h = (_rms_unit(x) * gain).astype(_BF16)
    scores = _dot(h, qk)
    probs = []
    for hd in range(MEM_HEADS):
        s = scores[:, hd * MEM_LEN:(hd + 1) * MEM_LEN]
        p = jnp.exp(s - jnp.max(s, axis=-1, keepdims=True))
        probs.append((p * (1.0 / jnp.sum(p, axis=-1, keepdims=True))).astype(_BF16))
    return x + _dot(jnp.concatenate(probs, axis=1), vo)


def _convmix_kernel(x_ref, g_ref, win_hbm, cw_ref, wout_hbm, gmem_ref, qk_ref, vo_ref, o_ref,
                    win_ref, wout_ref, carry_ref, *, tiles_per_seq):
    d = D_MODEL
    _stage_weights(pl.program_id(0) == 0, [(win_hbm.at[0], win_ref), (wout_hbm.at[0], wout_ref)])

    @pl.when(pl.program_id(0) % tiles_per_seq == 0)
    def _():
        carry_ref[...] = jnp.zeros_like(carry_ref)

    for rows in _sub_tiles(x_ref.shape[0]):
        x = x_ref[rows, :]
        h = (_rms_unit(x) * g_ref[0]).astype(_BF16)
        b_gate = _dot(h, win_ref[:, 0:d])
        z = _dot(h, win_ref[:, d:2 * d]) * _dot(h, win_ref[:, 2 * d:3 * d])
        prev = carry_ref[...]
        carry_ref[...] = z[z.shape[0] - V7X_SUBLANES:, :]
        conv = (cw_ref[0, 2:3, :] * z
                + cw_ref[0, 1:2, :] * _shift_rows(z, prev, 1)
                + cw_ref[0, 0:1, :] * _shift_rows(z, prev, 2))
        y = (b_gate * conv).astype(_BF16)
        x = x + _dot(y, wout_ref[...])
        o_ref[rows, :] = _mem_attention(x, gmem_ref[0], qk_ref[0, 0], vo_ref[0, 0])


def _convmix(x2d, norm_mix, w_in, w_conv, w_out, norm_mem, qk, vo, seq):
    n = x2d.shape[0]
    tm = TOKEN_TILE
    tiles_per_seq = seq // tm
    cols = MEM_HEADS * MEM_LEN
    blk = 2 * tm * D_MODEL * 4 + 2 * D_MODEL * cols * 2
    resident = 4 * D_MODEL * D_MODEL * 2 + STAGE_BYTES + 14 * SUB_TILE * D_MODEL * 4
    return pl.pallas_call(
        functools.partial(_convmix_kernel, tiles_per_seq=tiles_per_seq),
        grid=(n // tm,),
        in_specs=[
            pl.BlockSpec((tm, D_MODEL), lambda i: (i, 0)),
            _gain_spec(0),
            _HBM,
            pl.BlockSpec((1, CONV_W, D_MODEL), lambda i: (0, 0, 0)),
            _HBM,
            _gain_spec(0),
            pl.BlockSpec((1, 1, D_MODEL, cols), lambda i: (0, i // tiles_per_seq, 0, 0)),
            pl.BlockSpec((1, 1, cols, D_MODEL), lambda i: (0, i // tiles_per_seq, 0, 0)),
        ],
        out_specs=pl.BlockSpec((tm, D_MODEL), lambda i: (i, 0)),
        out_shape=jax.ShapeDtypeStruct((n, D_MODEL), _F32),
        scratch_shapes=[
            pltpu.VMEM((D_MODEL, 3 * D_MODEL), _BF16),
            pltpu.VMEM((D_MODEL, D_MODEL), _BF16),
            pltpu.VMEM((V7X_SUBLANES, D_MODEL), _F32),
        ],
        compiler_params=_params(blk, resident, 1),
        name="convmix_memattn",
    )(x2d, norm_mix.reshape(DEPTH, 1, D_MODEL), w_in, w_conv, w_out,
      norm_mem.reshape(DEPTH, 1, D_MODEL), qk, vo)


def _memattn_kernel(*refs, has_pre):
    if has_pre:
        x_ref, a_ref, wpre_hbm, g_ref, qk_ref, vo_ref, o_ref, wpre_ref = refs
        _stage_weights(pl.program_id(0) == 0, [(wpre_hbm.at[0], wpre_ref)])
    else:
        x_ref, g_ref, qk_ref, vo_ref, o_ref = refs
    for rows in _sub_tiles(x_ref.shape[0]):
        x = x_ref[rows, :]
        if has_pre:
            x = x + _dot(a_ref[rows, :], wpre_ref[...])
        o_ref[rows, :] = _mem_attention(x, g_ref[0], qk_ref[0, 0], vo_ref[0, 0])


def _memattn(x2d, norm_mem, qk, vo, layer, seq, pre=None):
    n = x2d.shape[0]
    tm = TOKEN_TILE
    tiles_per_seq = seq // tm
    cols = MEM_HEADS * MEM_LEN
    row_spec = pl.BlockSpec((tm, D_MODEL), lambda i: (i, 0))
    in_specs = [row_spec]
    args = [x2d]
    scratch = []
    if pre is not None:
        in_specs += [row_spec, _HBM]
        args += list(pre)
        scratch.append(pltpu.VMEM((D_MODEL, D_MODEL), _BF16))
    in_specs += [
        _gain_spec(layer),
        pl.BlockSpec((1, 1, D_MODEL, cols), lambda i: (layer, i // tiles_per_seq, 0, 0)),
        pl.BlockSpec((1, 1, cols, D_MODEL), lambda i: (layer, i // tiles_per_seq, 0, 0)),
    ]
    args += [norm_mem.reshape(DEPTH, 1, D_MODEL), qk, vo]
    blk = 3 * tm * D_MODEL * 4 + 2 * D_MODEL * cols * 2
    resident = len(scratch) * D_MODEL * D_MODEL * 2 + STAGE_BYTES + 8 * SUB_TILE * D_MODEL * 4
    return pl.pallas_call(
        functools.partial(_memattn_kernel, has_pre=pre is not None),
        grid=(n // tm,),
        in_specs=in_specs,
        out_specs=row_spec,
        out_shape=jax.ShapeDtypeStruct((n, D_MODEL), _F32),
        scratch_shapes=scratch,
        compiler_params=_params(blk, resident, 1),
        name="memattn_l%d" % layer,
    )(*args)


def _ff_chunks():
    return [(c0, min(FF_CHUNK, D_FF - c0)) for c0 in range(0, D_FF, FF_CHUNK)]


def _swiglu_kernel(*refs, final, layer):
    if final:
        x_ref, g_ref, wgu_hbm, wd_hbm, gf_ref, o_ref, wgu_ref, wd_ref = refs
    else:
        x_ref, g_ref, wgu_hbm, wd_hbm, o_ref, wgu_ref, wd_ref = refs
    _stage_weights(pl.program_id(0) == 0, [(wgu_hbm.at[layer], wgu_ref), (wd_hbm.at[layer], wd_ref)])
    for rows in _sub_tiles(x_ref.shape[0], FFN_SUB_TILE):
        x = x_ref[rows, :]
        h = (_rms_unit(x) * g_ref[0]).astype(_BF16)
        acc = x
        for c0, cw in _ff_chunks():
            gate = _dot(h, wgu_ref[:, c0:c0 + cw])
            up = _dot(h, wgu_ref[:, D_FF + c0:D_FF + c0 + cw])
            act = (gate * (1.0 / (1.0 + jnp.exp(-gate))) * up).astype(_BF16)
            acc = acc + _dot(act, wd_ref[c0:c0 + cw, :])
        if final:
            acc = _rms_unit(acc) * gf_ref[0]
        o_ref[rows, :] = acc


def _swiglu(x2d, norm_ffn, w_gu, w_down, layer, final_gain=None):
    n = x2d.shape[0]
    tm = TOKEN_TILE
    in_specs = [pl.BlockSpec((tm, D_MODEL), lambda i: (i, 0)), _gain_spec(layer), _HBM, _HBM]
    args = [x2d, norm_ffn.reshape(DEPTH, 1, D_MODEL), w_gu, w_down]
    if final_gain is not None:
        in_specs.append(_gain_spec(0))
        args.append(final_gain.reshape(1, 1, D_MODEL))
    blk = 2 * tm * D_MODEL * 4
    resident = (3 * D_MODEL * D_FF * 2 + STAGE_BYTES
                + 4 * FFN_SUB_TILE * D_MODEL * 4 + 4 * FFN_SUB_TILE * FF_CHUNK * 4)
    return pl.pallas_call(
        functools.partial(_swiglu_kernel, final=final_gain is not None, layer=layer),
        grid=(n // tm,),
        in_specs=in_specs,
        out_specs=pl.BlockSpec((tm, D_MODEL), lambda i: (i, 0)),
        out_shape=jax.ShapeDtypeStruct((n, D_MODEL), _F32),
        scratch_shapes=[pltpu.VMEM((D_MODEL, 2 * D_FF), _BF16), pltpu.VMEM((D_FF, D_MODEL), _BF16)],
        compiler_params=_params(blk, resident, 1),
        name="swiglu_final" if final_gain is not None else "swiglu",
    )(*args)


def _rope(x, cos, sin_signed, first_half):
    d = x.shape[1]
    rot = jnp.where(first_half, pltpu.roll(x, d - ROT_DIM // 2, 1), pltpu.roll(x, ROT_DIM // 2, 1))
    return x * cos + rot * sin_signed


def _qkv_kernel(x_ref, pos_ref, invf_ref, gq_ref, gk_ref, wq_hbm, wkv_hbm,
                q_ref, k_ref, vt_ref, km_ref, wq_ref, wkv_ref, *, tiles_per_seq):
    d = D_MODEL
    tm = x_ref.shape[0]
    blocks_per_tile = tm // BLOCK
    s_tile = pl.program_id(0) % tiles_per_seq
    _stage_weights(pl.program_id(0) == 0, [(wq_hbm.at[0], wq_ref), (wkv_hbm, wkv_ref)])

    @pl.when(s_tile == 0)
    def _():
        km_ref[...] = jnp.zeros_like(km_ref)

    lane = lax.broadcasted_iota(jnp.int32, (1, HEAD_DIM), 1)
    lane_d = lax.broadcasted_iota(jnp.int32, (1, d), 1)
    first_half = (lane_d % HEAD_DIM) < ROT_DIM // 2
    row16 = lax.broadcasted_iota(jnp.int32, km_ref.shape[1:], 0)

    for rows in _sub_tiles(tm):
        xh = _rms_unit(x_ref[rows, :])
        hq = (xh * gq_ref[0]).astype(_BF16)
        hk = (xh * gk_ref[0]).astype(_BF16)

        pos = pos_ref[0, :, rows].astype(_F32)
        pos_rows = jnp.broadcast_to(pos, (V7X_LANES, SUB_TILE)).T
        ang = pos_rows * invf_ref[...]
        cos = jnp.cos(ang)
        sin_signed = jnp.where(lane < ROT_DIM // 2, -1.0, 1.0) * jnp.sin(ang)
        cos = jnp.concatenate([cos] * N_HEADS, axis=1)
        sin_signed = jnp.concatenate([sin_signed] * N_HEADS, axis=1)

        q = _dot(hq, wq_ref[...]) * (HEAD_DIM ** -0.5 * math.log2(math.e))
        q_ref[rows, :] = _rope(q, cos, sin_signed, first_half).astype(_BF16)

        k = _rope(_dot(hk, wkv_ref[:, 0:d]), cos, sin_signed, first_half)
        k_ref[rows, :] = k.astype(_BF16)
        v = _dot(hk, wkv_ref[:, d:2 * d])

        for r0 in range(0, SUB_TILE, BLOCK):
            j = (rows.start + r0) // BLOCK
            vt_ref[0, j] = v[r0:r0 + BLOCK, :].T.astype(_BF16)
            mean_row = jnp.sum(k[r0:r0 + BLOCK, :], axis=0, keepdims=True) * (1.0 / BLOCK)
            km_ref[0] = jnp.where(row16 == s_tile * blocks_per_tile + j, mean_row, km_ref[0])


def _qkv(x2d, positions, norm_mix, kv_norm, w_q, w_kv, bsz, seq):
    n = x2d.shape[0]
    tm = QKV_TOKEN_TILE
    tiles_per_seq = seq // tm
    n_blocks = seq // BLOCK
    inv_freq = ROPE_THETA ** (-jnp.arange(0, ROT_DIM, 2, dtype=_F32) / ROT_DIM)
    invf_lane = jnp.concatenate(
        [inv_freq, inv_freq, jnp.zeros((HEAD_DIM - ROT_DIM,), _F32)]).reshape(1, HEAD_DIM)
    row_spec = pl.BlockSpec((tm, D_MODEL), lambda i: (i, 0))
    blk = tm * D_MODEL * 4 + 3 * tm * D_MODEL * 2 + n_blocks * D_MODEL * 4
    resident = 3 * D_MODEL * D_MODEL * 2 + STAGE_BYTES + 12 * SUB_TILE * D_MODEL * 4
    return pl.pallas_call(
        functools.partial(_qkv_kernel, tiles_per_seq=tiles_per_seq),
        grid=(n // tm,),
        in_specs=[
            row_spec,
            pl.BlockSpec((1, 1, tm), lambda i: (i, 0, 0)),
            pl.BlockSpec((1, HEAD_DIM), lambda i: (0, 0)),
            _gain_spec(1),
            _gain_spec(0),
            _HBM,
            _HBM,
        ],
        out_specs=[
            row_spec,
            row_spec,
            pl.BlockSpec((1, tm // BLOCK, D_MODEL, BLOCK),
                         lambda i: (i // tiles_per_seq, i % tiles_per_seq, 0, 0)),
            pl.BlockSpec((1, n_blocks, D_MODEL), lambda i: (i // tiles_per_seq, 0, 0)),
        ],
        out_shape=[
            jax.ShapeDtypeStruct((n, D_MODEL), _BF16),
            jax.ShapeDtypeStruct((n, D_MODEL), _BF16),
            jax.ShapeDtypeStruct((bsz, n_blocks, D_MODEL, BLOCK), _BF16),
            jax.ShapeDtypeStruct((bsz, n_blocks, D_MODEL), _F32),
        ],
        scratch_shapes=[pltpu.VMEM((D_MODEL, D_MODEL), _BF16), pltpu.VMEM((D_MODEL, 2 * D_MODEL), _BF16)],
        compiler_params=_params(blk, resident, 1),
        name="qkv",
    )(x2d, positions.reshape(n // tm, 1, tm), invf_lane,
      norm_mix.reshape(DEPTH, 1, D_MODEL), kv_norm.reshape(1, 1, D_MODEL), w_q, w_kv)


def _moba_kernel(q_ref, k_hbm, vt_hbm, km_ref, o_ref, k_ref, vt_ref, kv_sem, bias_ref, m_ref, l_ref, acc_ref):
    g = pl.program_id(0)
    t = pl.program_id(1)
    n_blocks = km_ref.shape[2]
    own_rows = pl.ds(pl.multiple_of(t * BLOCK, BLOCK), BLOCK)
    chains = [(a, slice(h * HEAD_DIM, (h + 1) * HEAD_DIM))
              for a in range(q_ref.shape[0]) for h in range(N_HEADS)]

    def kv_block_copies(group, block):
        rows = pl.ds(pl.multiple_of(block * BLOCK, BLOCK), BLOCK)
        return (pltpu.make_async_copy(k_hbm.at[:, group, rows, :], k_ref.at[:, rows, :], kv_sem.at[0]),
                pltpu.make_async_copy(vt_hbm.at[:, group, block], vt_ref.at[:, block], kv_sem.at[1]))

    @pl.when(jnp.logical_and(g == 0, t == 0))
    def _():
        for cp in kv_block_copies(0, 0):
            cp.start()

    for cp in kv_block_copies(g, t):
        cp.wait()

    @pl.when(t + 1 < n_blocks)
    def _():
        for cp in kv_block_copies(g, t + 1):
            cp.start()

    def score_t(i, rows):
        a, c = chains[i]
        return lax.dot_general(k_ref[a, rows, c], q_ref[a, 0, :, c], _NT, preferred_element_type=_F32)

    gates = []
    for a, c in chains:
        km = km_ref[a, 0, :, c]
        km_hi = km.astype(_BF16)
        km_lo = (km - km_hi.astype(_F32)).astype(_BF16)
        gates.append(lax.dot_general(km_hi, q_ref[a, 0, :, c], _NT, preferred_element_type=_F32)
                     + lax.dot_general(km_lo, q_ref[a, 0, :, c], _NT, preferred_element_type=_F32))
    own_scores = [score_t(i, own_rows) for i in range(len(chains))]

    blk = lax.broadcasted_iota(jnp.int32, (n_blocks, BLOCK), 0).astype(_F32)
    past = blk < t.astype(_F32)
    for i, gate in enumerate(gates):
        in_play = jnp.where(past, gate, -jnp.inf)
        bias = jnp.full(gate.shape, -jnp.inf, _F32)
        for _ in range(TOPK):
            best = jnp.max(in_play, axis=0, keepdims=True)
            first = jnp.min(jnp.where(in_play == best, blk, float(n_blocks)), axis=0, keepdims=True)
            taken = jnp.where(blk == first, best, -jnp.inf) > -jnp.inf
            bias = jnp.where(taken, 0.0, bias)
            in_play = jnp.where(taken, -jnp.inf, in_play)
        bias_ref[i] = bias

    ones_rows = jnp.ones((_BF16_SUBLANES, BLOCK), _BF16)

    kpos = lax.broadcasted_iota(jnp.int32, (BLOCK, BLOCK), 0)
    qpos = lax.broadcasted_iota(jnp.int32, (BLOCK, BLOCK), 1)
    for i, (a, c) in enumerate(chains):
        s = jnp.where(kpos <= qpos, own_scores[i], -jnp.inf)
        m0 = jnp.max(s, axis=0, keepdims=True)
        p = jnp.exp2(s - m0).astype(_BF16)
        pv = _dot(jnp.concatenate([vt_ref[a, t, c, :], ones_rows], axis=0), p)
        m_ref[i:i + 1, :] = m0
        l_ref[i:i + 1, :] = pv[HEAD_DIM:HEAD_DIM + 1]
        acc_ref[i] = pv[:HEAD_DIM]

    def past_blocks(n0, count):
        rows = pl.ds(pl.multiple_of(n0 * BLOCK, BLOCK), count * BLOCK)
        scores = {i: score_t(i, rows) for i in range(MOBA_SCORE_LOOKAHEAD)}
        for i, (a, c) in enumerate(chains):
            s = scores.pop(i)
            s_blk = [s[b * BLOCK:(b + 1) * BLOCK] for b in range(count)]
            bias = [bias_ref[i, pl.ds(n0 + b, 1), :] for b in range(count)]
            m_run = m_ref[i:i + 1, :]
            m_new = m_run
            for b in range(count):
                m_new = jnp.maximum(m_new, jnp.max(s_blk[b], axis=0, keepdims=True) + bias[b])
            m_ref[i:i + 1, :] = m_new
            alpha = jnp.exp2(m_run - m_new)
            pv = None
            for b in range(count):
                p = jnp.exp2(s_blk[b] - (m_new - bias[b])).astype(_BF16)
                term = _dot(jnp.concatenate([vt_ref[a, n0 + b, c, :], ones_rows], axis=0), p)
                pv = term if pv is None else pv + term
            if i + MOBA_SCORE_LOOKAHEAD < len(chains):
                scores[i + MOBA_SCORE_LOOKAHEAD] = score_t(i + MOBA_SCORE_LOOKAHEAD, rows)
            acc_ref[i] = alpha * acc_ref[i] + pv[:HEAD_DIM]
            l_ref[i:i + 1, :] = alpha * l_ref[i:i + 1, :] + pv[HEAD_DIM:HEAD_DIM + 1]

    def past_block_pair(j, carry):
        past_blocks(2 * j, 2)
        return carry

    lax.fori_loop(0, lax.shift_right_logical(t, 1), past_block_pair, 0)

    @pl.when(lax.rem(t, 2) == 1)
    def _():
        past_blocks(t - 1, 1)

    for i, (a, c) in enumerate(chains):
        out = acc_ref[i] * (1.0 / l_ref[i:i + 1, :])
        o_ref[a, 0, :, c] = out.T.astype(_BF16)

    @pl.when(jnp.logical_and(t == n_blocks - 1, g + 1 < pl.num_programs(0)))
    def _():
        for cp in kv_block_copies(g + 1, 0):
            cp.start()


def _moba(q, k, vt, k_mean, bsz, seq):
    n_blocks = seq // BLOCK
    bps = MOBA_BATCHES_PER_STEP
    groups = bsz // bps
    n_chains = bps * N_HEADS
    q4 = q.reshape(bps, groups, seq, D_MODEL)
    k4 = k.reshape(bps, groups, seq, D_MODEL)
    vt5 = vt.reshape(bps, groups, n_blocks, D_MODEL, BLOCK)
    km4 = k_mean.reshape(bps, groups, n_blocks, D_MODEL)
    blk = bps * (2 * BLOCK * D_MODEL * 2 + n_blocks * D_MODEL * 4)
    scratch = 2 * bps * seq * D_MODEL * 2 + n_chains * (n_blocks + HEAD_DIM + 2) * BLOCK * 4
    out = pl.pallas_call(
        _moba_kernel,
        grid=(groups, n_blocks),
        in_specs=[
            pl.BlockSpec((bps, 1, BLOCK, D_MODEL), lambda g, t: (0, g, t, 0)),
            _HBM,
            _HBM,
            pl.BlockSpec((bps, 1, n_blocks, D_MODEL), lambda g, t: (0, g, 0, 0)),
        ],
        out_specs=pl.BlockSpec((bps, 1, BLOCK, D_MODEL), lambda g, t: (0, g, t, 0)),
        out_shape=jax.ShapeDtypeStruct((bps, groups, seq, D_MODEL), _BF16),
        scratch_shapes=[
            pltpu.VMEM((bps, seq, D_MODEL), _BF16),
            pltpu.VMEM((bps, n_blocks, D_MODEL, BLOCK), _BF16),
            pltpu.SemaphoreType.DMA((2,)),
            pltpu.VMEM((n_chains, n_blocks, BLOCK), _F32),
            pltpu.VMEM((n_chains, BLOCK), _F32),
            pltpu.VMEM((n_chains, BLOCK), _F32),
            pltpu.VMEM((n_chains, HEAD_DIM, BLOCK), _F32),
        ],
        compiler_params=_params(blk, scratch + n_chains * 6 * BLOCK * BLOCK * 4, 2),
        name="moba",
    )(q4, k4, vt5, km4)
    return out.reshape(bsz * seq, D_MODEL)


def kernel(x, mem, positions, norm_mix, norm_mem, norm_memkv, norm_ffn, norm_final,
           conv_w_in, conv_w, conv_w_out, kv_norm, w_kv, moba_w_q, moba_w_o,
           mem_w_q, mem_w_kv, mem_w_o, ffn_w_gu, ffn_w_down):
    bsz, seq, d = x.shape
    assert d == D_MODEL and seq % TOKEN_TILE == 0 and seq % QKV_TOKEN_TILE == 0
    assert TOKEN_TILE % SUB_TILE == 0 and QKV_TOKEN_TILE % SUB_TILE == 0 and SUB_TILE % BLOCK == 0
    assert TOKEN_TILE % FFN_SUB_TILE == 0
    assert mem.shape == (bsz, MEM_LEN, D_MODEL)
    assert bsz % MOBA_BATCHES_PER_STEP == 0
    assert norm_mix.shape[0] == DEPTH == 2 and conv_w_in.shape[0] == 1 and moba_w_q.shape[0] == 1

    x2d = x.reshape(bsz * seq, d)
    mem_qk, mem_vo = _memkv(mem, norm_memkv, mem_w_kv, mem_w_q, mem_w_o)

    x2d = _convmix(x2d, norm_mix, conv_w_in, conv_w, conv_w_out, norm_mem, mem_qk, mem_vo, seq)
    x2d = _swiglu(x2d, norm_ffn, ffn_w_gu, ffn_w_down, 0)

    q, k, vt, k_mean = _qkv(x2d, positions, norm_mix, kv_norm, moba_w_q, w_kv, bsz, seq)
    attn = _moba(q, k, vt, k_mean, bsz, seq)

    x2d = _memattn(x2d, norm_mem, mem_qk, mem_vo, 1, seq, pre=(attn, moba_w_o))
    x2d = _swiglu(x2d, norm_ffn, ffn_w_gu, ffn_w_down, 1, final_gain=norm_final)
    return x2d.reshape(bsz, seq, d)
```
